```python
import math
import jax, jax.numpy as jnp
from jax import lax
import numpy as np

D_MODEL = 2048
BATCH = 4
SEQ = 2048
DEPTH = 2

N_MEM = 256
N_MIXERS = 2
EXPAND = 2
INNER = EXPAND * D_MODEL
XA_HEADS = 4
XA_WIDTH = INNER // 4
XA_DIM = XA_WIDTH // XA_HEADS
MIX_WIDTH = INNER - XA_WIDTH
DN_HEAD_DIM = 128
DN_V_HEADS = MIX_WIDTH // DN_HEAD_DIM
DN_QK_HEADS = DN_V_HEADS // 2
DN_QK_WIDTH = DN_QK_HEADS * DN_HEAD_DIM
DN_CONV = 4
DN_CHUNK = 64
DN_MIX_COLS = 2 * DN_QK_WIDTH + MIX_WIDTH + 2 * DN_V_HEADS
DN_PROJ = DN_MIX_COLS + XA_WIDTH + INNER
SB_HEAD_DIM = 128
SB_HEADS = MIX_WIDTH // SB_HEAD_DIM
SB_BLOCK = 128
SB_MIX_COLS = 3 * MIX_WIDTH
SB_PROJ = SB_MIX_COLS + XA_WIDTH + INNER
EPS = 1e-6

kernel_name = "hybrid_deltanet_stickbreaking_memxattn"


def rms_norm(x, g):
    xf = x.astype(jnp.float32)
    y = xf * lax.rsqrt(jnp.mean(xf * xf, axis=-1, keepdims=True) + EPS)
    return (y * g.astype(jnp.float32)).astype(x.dtype)


def l2_norm(x):
    xf = x.astype(jnp.float32)
    return (xf * lax.rsqrt(jnp.sum(xf * xf, axis=-1, keepdims=True) + EPS)).astype(x.dtype)


def causal_depthwise_conv(x, w):
    c = x.shape[-1]
    return lax.conv_general_dilated(
        x, w[:, None, :].astype(x.dtype), window_strides=(1,),
        padding=((w.shape[0] - 1, 0),), dimension_numbers=("NWC", "WIO", "NWC"),
        feature_group_count=c)


def gated_delta_rule(q, k, v, g, beta):
    f32 = jnp.float32
    B, H, S, dk = q.shape
    dv = v.shape[-1]
    C, N = DN_CHUNK, S // DN_CHUNK
    q, k, v = (t.astype(f32).reshape(B, H, N, C, t.shape[-1]) for t in (q, k, v))
    g = g.astype(f32).reshape(B, H, N, C)
    beta = beta.astype(f32).reshape(B, H, N, C)
    gc = jnp.cumsum(g, axis=-1)
    idx = jnp.arange(C)
    incl = idx[:, None] >= idx[None, :]
    strict = idx[:, None] > idx[None, :]
    decay = jnp.exp(jnp.where(incl, gc[..., :, None] - gc[..., None, :], -jnp.inf))
    kk = jnp.einsum("bhncd,bhnjd->bhncj", k, k)
    m = jnp.where(strict, beta[..., :, None] * kk * decay, 0.0) + jnp.eye(C, dtype=f32)
    rhs = jnp.concatenate([v * beta[..., None], k * (beta * jnp.exp(gc))[..., None]], axis=-1)
    sol = lax.linalg.triangular_solve(m, rhs, left_side=True, lower=True, unit_diagonal=True)
    u0, w = sol[..., :dv], sol[..., dv:]
    qk = jnp.einsum("bhncd,bhnjd->bhncj", q, k) * decay
    q_dec = q * jnp.exp(gc)[..., None]
    k_dec = k * jnp.exp(gc[..., -1:] - gc)[..., None]
    chunk_decay = jnp.exp(gc[..., -1])

    def step(state, inp):
        u0_c, w_c, qk_c, qd_c, kd_c, cd_c = inp
        u = u0_c - jnp.einsum("bhcd,bhde->bhce", w_c, state)
        o = jnp.einsum("bhcd,bhde->bhce", qd_c, state) + jnp.einsum("bhcj,bhje->bhce", qk_c, u)
        state = cd_c[..., None, None] * state + jnp.einsum("bhcd,bhce->bhde", kd_c, u)
        return state, o

    xs = tuple(jnp.moveaxis(t, 2, 0) for t in (u0, w, qk, q_dec, k_dec, chunk_decay))
    _, o = lax.scan(step, jnp.zeros((B, H, dk, dv), f32), xs)
    return jnp.moveaxis(o, 0, 2).reshape(B, H, S, dv)


def deltanet_branch(p, conv_w, a_log, dt_bias, out_g):
    B, S, _ = p.shape
    c_qkv = 2 * DN_QK_WIDTH + MIX_WIDTH
    qkv = jax.nn.silu(causal_depthwise_conv(p[..., :c_qkv], conv_w))
    q = qkv[..., :DN_QK_WIDTH].reshape(B, S, DN_QK_HEADS, DN_HEAD_DIM)
    k = qkv[..., DN_QK_WIDTH:2 * DN_QK_WIDTH].reshape(B, S, DN_QK_HEADS, DN_HEAD_DIM)
    v = qkv[..., 2 * DN_QK_WIDTH:].reshape(B, S, DN_V_HEADS, DN_HEAD_DIM)
    a = p[..., c_qkv:c_qkv + DN_V_HEADS].astype(jnp.float32)
    b = p[..., c_qkv + DN_V_HEADS:].astype(jnp.float32)
    rep = DN_V_HEADS // DN_QK_HEADS
    q = jnp.repeat(l2_norm(q), rep, axis=2) * DN_HEAD_DIM ** -0.5
    k = jnp.repeat(l2_norm(k), rep, axis=2)
    g = -jnp.exp(a_log.astype(jnp.float32)) * jax.nn.softplus(a + dt_bias.astype(jnp.float32))
    beta = jax.nn.sigmoid(b)
    o = gated_delta_rule(q.transpose(0, 2, 1, 3), k.transpose(0, 2, 1, 3), v.transpose(0, 2, 1, 3),
                         g.transpose(0, 2, 1), beta.transpose(0, 2, 1))
    o = rms_norm(o.astype(p.dtype), out_g)
    return o.transpose(0, 2, 1, 3).reshape(B, S, MIX_WIDTH)


def stick_breaking_attention(q, k, v):
    B, H, S, d = q.shape
    scale = d ** -0.5
    outs = []
    for blk in range(S // SB_BLOCK):
        q0 = blk * SB_BLOCK
        kv_len = q0 + SB_BLOCK
        qb = q[:, :, q0:kv_len]
        kb, vb = k[:, :, :kv_len], v[:, :, :kv_len]
        z = jnp.einsum("bhtd,bhsd->bhts", qb, kb).astype(jnp.float32) * scale
        t_pos = q0 + jnp.arange(SB_BLOCK)
        s_pos = jnp.arange(kv_len)
        mask = s_pos[None, :] < t_pos[:, None]
        log_rest = jnp.where(mask, jax.nn.log_sigmoid(-z), 0.0)
        later = lax.cumsum(log_rest, axis=3, reverse=True) - log_rest
        wts = jnp.where(mask, jnp.exp(jax.nn.log_sigmoid(z) + later), 0.0)
        outs.append(jnp.einsum("bhts,bhsd->bhtd", wts.astype(v.dtype), vb))
    return jnp.concatenate(outs, axis=2)


def stick_breaking_branch(p, qn_g, kn_g):
    B, S, _ = p.shape
    q, k, v = (t.reshape(B, S, SB_HEADS, SB_HEAD_DIM) for t in jnp.split(p, 3, axis=-1))
    q, k = rms_norm(q, qn_g), rms_norm(k, kn_g)
    o = stick_breaking_attention(q.transpose(0, 2, 1, 3), k.transpose(0, 2, 1, 3), v.transpose(0, 2, 1, 3))
    return o.transpose(0, 2, 1, 3).reshape(B, S, MIX_WIDTH)


def memory_cross_attention(xq, mem_n, w_kv, qn_g, kn_g):
    B, S, _ = xq.shape
    q = rms_norm(xq.reshape(B, S, XA_HEADS, XA_DIM), qn_g)
    kv = jnp.einsum("bmd,de->bme", mem_n, w_kv)
    k = rms_norm(kv[..., :XA_WIDTH].reshape(B, -1, XA_HEADS, XA_DIM), kn_g)
    v = kv[..., XA_WIDTH:].reshape(B, -1, XA_HEADS, XA_DIM)
    s = jnp.einsum("bthd,bmhd->bhtm", q, k).astype(jnp.float32) * XA_DIM ** -0.5
    p = jax.nn.softmax(s, axis=-1).astype(v.dtype)
    return jnp.einsum("bhtm,bmhd->bthd", p, v).reshape(B, S, XA_WIDTH)


def setup_inputs(seed: int = 0) -> dict:
    key = jax.random.key(seed)
    ks = jax.random.split(key, 20)
    f32 = jnp.float32
    n_dn = (DEPTH + N_MIXERS - 1) // N_MIXERS
    n_sb = DEPTH // N_MIXERS

    def dense(k, shape, fan_in):
        return jax.random.normal(k, shape, f32) * fan_in ** -0.5

    def gain(k, shape):
        return 1.0 + 0.02 * jax.random.normal(k, shape, f32)

    dt = jnp.exp(jax.random.uniform(ks[10], (n_dn, DN_V_HEADS), f32,
                                    minval=math.log(1e-3), maxval=math.log(1e-1)))
    return {
        "x": jax.random.normal(ks[0], (BATCH, SEQ, D_MODEL), f32),
        "mem": jax.random.normal(ks[1], (BATCH, N_MEM, D_MODEL), f32),
        "norm_g": gain(ks[2], (DEPTH, D_MODEL)),
        "mem_norm_g": gain(ks[3], (D_MODEL,)),
        "mem_w_kv": dense(ks[4], (DEPTH, D_MODEL, 2 * XA_WIDTH), D_MODEL),
        "xa_q_norm_g": gain(ks[5], (DEPTH, XA_DIM)),
        "xa_k_norm_g": gain(ks[6], (DEPTH, XA_DIM)),
        "w_out": dense(ks[7], (DEPTH, INNER, D_MODEL), INNER),
        "dn_w_in": dense(ks[8], (n_dn, D_MODEL, DN_PROJ), D_MODEL),
        "dn_conv_w": dense(ks[9], (n_dn, DN_CONV, 2 * DN_QK_WIDTH + MIX_WIDTH), DN_CONV),
        "dn_a_log": jnp.log(jax.random.uniform(ks[11], (n_dn, DN_V_HEADS), f32, minval=1.0, maxval=16.0)),
        "dn_dt_bias": dt + jnp.log(-jnp.expm1(-dt)),
        "dn_out_norm_g": gain(ks[12], (n_dn, DN_HEAD_DIM)),
        "sb_w_in": dense(ks[13], (n_sb, D_MODEL, SB_PROJ), D_MODEL),
        "sb_q_norm_g": gain(ks[14], (n_sb, SB_HEAD_DIM)),
        "sb_k_norm_g": gain(ks[15], (n_sb, SB_HEAD_DIM)),
    }


def reference(x, mem, norm_g, mem_norm_g, mem_w_kv, xa_q_norm_g, xa_k_norm_g, w_out,
              dn_w_in, dn_conv_w, dn_a_log, dn_dt_bias, dn_out_norm_g,
              sb_w_in, sb_q_norm_g, sb_k_norm_g):
    mem_n = rms_norm(mem, mem_norm_g)
    for i in range(DEPTH):
        h = rms_norm(x, norm_g[i])
        j = i // N_MIXERS
        if i % N_MIXERS == 0:
            proj = jnp.einsum("bsd,de->bse", h, dn_w_in[j])
            mix = deltanet_branch(proj[..., :DN_MIX_COLS], dn_conv_w[j], dn_a_log[j],
                                  dn_dt_bias[j], dn_out_norm_g[j])
        else:
            proj = jnp.einsum("bsd,de->bse", h, sb_w_in[j])
            mix = stick_breaking_branch(proj[..., :SB_MIX_COLS], sb_q_norm_g[j], sb_k_norm_g[j])
        xq = proj[..., -(XA_WIDTH + INNER):-INNER]
        z = proj[..., -INNER:]
        xa = memory_cross_attention(xq, mem_n, mem_w_kv[i], xa_q_norm_g[i], xa_k_norm_g[i])
        y = jnp.concatenate([mix, xa], axis=-1) * jax.nn.silu(z)
        x = x + jnp.einsum("bse,ed->bsd", y, w_out[i])
    return x
```

```python
import functools

import jax
import jax.numpy as jnp
from jax import lax
from jax.experimental import pallas as pl
from jax.experimental.pallas import tpu as pltpu

F32 = jnp.float32
BF16 = jnp.bfloat16
EPS = 1e-6

XA_HEADS = 4
XA_DIM = 256
XA_WIDTH = XA_HEADS * XA_DIM
HEAD_DIM = 128
DN_CONV = 4
DN_CHUNK = 64
DN_PAIR = 2 * DN_CHUNK
N_GATE_ROWS = 8

LANES = 128
VMEM_BYTES_V7X = 64 * 1024 * 1024
VMEM_LIMIT = VMEM_BYTES_V7X - 8 * 1024 * 1024

NEG_BIG = -1e30


def _cparams(sem):
    return pltpu.CompilerParams(dimension_semantics=sem, vmem_limit_bytes=VMEM_LIMIT)


def _dot(a, b):
    return jnp.dot(a, b, preferred_element_type=F32)


def _dot_nt(a, b):
    return lax.dot_general(a, b, (((1,), (1,)), ((), ())), preferred_element_type=F32)


def _split3(x):
    hi = x.astype(BF16)
    r = x - hi.astype(F32)
    mid = r.astype(BF16)
    lo = (r - mid.astype(F32)).astype(BF16)
    return hi, mid, lo


def _sigmoid(x):
    return 1.0 / (1.0 + jnp.exp(-x))


def _softplus(x):
    return jnp.maximum(x, 0.0) + jnp.log(1.0 + jnp.exp(-jnp.abs(x)))


def _iota2(shape, dim):
    return lax.broadcasted_iota(jnp.int32, shape, dim)


def _inproj_kernel(x_ref, g_ref, w_ref, o_ref, h_ref):
    @pl.when(pl.program_id(1) == 0)
    def _():
        x = x_ref[...]
        ms = jnp.mean(x * x, axis=-1, keepdims=True)
        h_ref[...] = ((x * lax.rsqrt(ms + EPS)) * g_ref[...]).astype(BF16)

    o_ref[...] = _dot(h_ref[...], w_ref[...]).astype(o_ref.dtype)


def _inproj(x2, g, w, tm, tn):
    m, d = x2.shape
    p = w.shape[1]
    return pl.pallas_call(
        _inproj_kernel,
        grid=(m // tm, p // tn),
        in_specs=[
            pl.BlockSpec((tm, d), lambda i, j: (i, 0)),
            pl.BlockSpec((1, d), lambda i, j: (0, 0)),
            pl.BlockSpec((d, tn), lambda i, j: (0, j)),
        ],
        out_specs=pl.BlockSpec((tm, tn), lambda i, j: (i, j)),
        out_shape=jax.ShapeDtypeStruct((m, p), BF16),
        scratch_shapes=[pltpu.VMEM((tm, d), BF16)],
        compiler_params=_cparams(("parallel", "arbitrary")),
        name="rmsnorm_inproj",
    )(x2, g.reshape(1, d), w)


def _memkv_kernel(mem_ref, g_ref, w_ref, kg_ref, o_ref, *, k_tiles):
    x = mem_ref[...]
    ms = jnp.mean(x * x, axis=-1, keepdims=True)
    mn = ((x * lax.rsqrt(ms + EPS)) * g_ref[...]).astype(BF16)
    kv = _dot(mn, w_ref[0])
    is_key = pl.program_id(1) < k_tiles

    @pl.when(is_key)
    def _():
        for c in range(kv.shape[1] // XA_DIM):
            kh = kv[:, c * XA_DIM:(c + 1) * XA_DIM]
            hs = jnp.mean(kh * kh, axis=-1, keepdims=True)
            o_ref[0, :, c * XA_DIM:(c + 1) * XA_DIM] = (
                (kh * lax.rsqrt(hs + EPS)) * kg_ref[0]).astype(o_ref.dtype)

    @pl.when(jnp.logical_not(is_key))
    def _():
        o_ref[0] = kv.astype(o_ref.dtype)


def _memkv(mem2, mem_g, w_kv, k_g, tn=512):
    depth, d, e = w_kv.shape
    mm = mem2.shape[0]
    return pl.pallas_call(
        functools.partial(_memkv_kernel, k_tiles=XA_WIDTH // tn),
        grid=(depth, e // tn),
        in_specs=[
            pl.BlockSpec((mm, d), lambda l, j: (0, 0)),
            pl.BlockSpec((1, d), lambda l, j: (0, 0)),
            pl.BlockSpec((1, d, tn), lambda l, j: (l, 0, j)),
            pl.BlockSpec((1, 1, XA_DIM), lambda l, j: (l, 0, 0)),
        ],
        out_specs=pl.BlockSpec((1, mm, tn), lambda l, j: (l, 0, j)),
        out_shape=jax.ShapeDtypeStruct((depth, mm, e), BF16),
        compiler_params=_cparams(("parallel", "parallel")),
        name="mem_kv",
    )(mem2, mem_g.reshape(1, d), w_kv, k_g.reshape(depth, 1, XA_DIM))


def _xattn_kernel(q_ref, qg_ref, k_ref, v_ref, o_ref):
    for h in range(XA_HEADS):
        cs = slice(h * XA_DIM, (h + 1) * XA_DIM)
        q = q_ref[:, cs].astype(F32)
        ms = jnp.mean(q * q, axis=-1, keepdims=True)
        qn = ((q * lax.rsqrt(ms + EPS)) * qg_ref[...]).astype(BF16)
        s = _dot_nt(qn, k_ref[0, :, cs]) * (XA_DIM ** -0.5)
        e = jnp.exp(s - jnp.max(s, axis=-1, keepdims=True))
        p = e / jnp.sum(e, axis=-1, keepdims=True)
        o_ref[:, cs] = _dot(p.astype(BF16), v_ref[0, :, cs]).astype(o_ref.dtype)


def _xattn(proj, kv, q_g, layer, batch, xq_off, tm=512):
    m = proj.shape[0]
    n_mem = kv.shape[1] // batch
    nt = m // batch // tm
    qb = xq_off // XA_WIDTH
    return pl.pallas_call(
        _xattn_kernel,
        grid=(batch, nt),
        in_specs=[
            pl.BlockSpec((tm, XA_WIDTH), lambda b, i: (b * nt + i, qb)),
            pl.BlockSpec((1, XA_DIM), lambda b, i: (0, 0)),
            pl.BlockSpec((1, n_mem, XA_WIDTH), lambda b, i: (layer, b, 0)),
            pl.BlockSpec((1, n_mem, XA_WIDTH), lambda b, i: (layer, b, 1)),
        ],
        out_specs=pl.BlockSpec((tm, XA_WIDTH), lambda b, i: (b * nt + i, 0)),
        out_shape=jax.ShapeDtypeStruct((m, XA_WIDTH), BF16),
        compiler_params=_cparams(("parallel", "parallel")),
        name="mem_xattn",
    )(proj, q_g.reshape(1, XA_DIM), kv, kv)


def _outproj_kernel(z_ref, mix_ref, xa_ref, x_ref, w_ref, o_ref):
    mw = mix_ref.shape[1]
    z = z_ref[...].astype(F32)
    sz = z * _sigmoid(z)
    y_mix = (mix_ref[...].astype(F32) * sz[:, :mw]).astype(BF16)
    y_xa = (xa_ref[...].astype(F32) * sz[:, mw:]).astype(BF16)
    o_ref[...] = x_ref[...] + _dot(y_mix, w_ref[:mw, :]) + _dot(y_xa, w_ref[mw:, :])


def _outproj(proj, mix, xa, x2, w, tm=256):
    m, d = x2.shape
    inner = w.shape[0]
    mw = mix.shape[1]
    return pl.pallas_call(
        _outproj_kernel,
        grid=(m // tm,),
        in_specs=[
            pl.BlockSpec((tm, inner), lambda i: (i, 0)),
            pl.BlockSpec((tm, mw), lambda i: (i, 0)),
            pl.BlockSpec((tm, inner - mw), lambda i: (i, 0)),
            pl.BlockSpec((tm, d), lambda i: (i, 0)),
            pl.BlockSpec((inner, d), lambda i: (0, 0), pipeline_mode=pl.Buffered(1)),
        ],
        out_specs=pl.BlockSpec((tm, d), lambda i: (i, 0)),
        out_shape=jax.ShapeDtypeStruct((m, d), F32),
        compiler_params=_cparams(("parallel",)),
        name="gate_outproj",
    )(proj, mix, xa, x2, w)


def _dn_gates_kernel(ab_ref, alog_ref, dtb_ref, o_ref, *, heads):
    s = ab_ref.shape[0]
    t = ab_ref[:, :LANES].astype(F32).T
    a = t[0:heads]
    b = t[heads:2 * heads]
    g = -jnp.exp(alog_ref[...]) * _softplus(a + dtb_ref[...])
    beta = _sigmoid(b)

    ii = _iota2((DN_PAIR, DN_PAIR), 0)
    jj = _iota2((DN_PAIR, DN_PAIR), 1)
    lo_i = ii < DN_CHUNK
    lo_j = jj < DN_CHUNK
    same = lo_i == lo_j
    one = lambda m: jnp.where(m, 1.0, 0.0).astype(BF16)
    rhs = jnp.concatenate(
        [one(same & (ii <= jj)), one(same), one(lo_i), one(jnp.logical_not(lo_i))], axis=1)

    for p in range(s // DN_PAIR):
        ls = slice(p * DN_PAIR, (p + 1) * DN_PAIR)
        hi, mid, lo = _split3(g[:, ls])
        r = _dot(hi, rhs) + _dot(mid, rhs) + _dot(lo, rhs)
        gc = r[:, 0:DN_PAIR]
        rows = (beta[:, ls], gc, r[:, DN_PAIR:2 * DN_PAIR] - gc,
                jnp.exp(r[:, 2 * DN_PAIR:3 * DN_PAIR]), jnp.exp(r[:, 3 * DN_PAIR:4 * DN_PAIR]))
        for n, val in enumerate(rows):
            o_ref[:, n, ls] = val
        for n in range(len(rows), N_GATE_ROWS):
            o_ref[:, n, ls] = jnp.zeros_like(gc)


def _dn_gates(proj, a_log, dt_bias, batch, ab_off):
    m = proj.shape[0]
    s = m // batch
    heads = a_log.shape[0]
    blk = 2 * LANES
    return pl.pallas_call(
        functools.partial(_dn_gates_kernel, heads=heads),
        grid=(batch,),
        in_specs=[
            pl.BlockSpec((s, blk), lambda b: (b, ab_off // blk)),
            pl.BlockSpec((heads, 1), lambda b: (0, 0)),
            pl.BlockSpec((heads, 1), lambda b: (0, 0)),
        ],
        out_specs=pl.BlockSpec((None, heads, N_GATE_ROWS, s), lambda b: (b, 0, 0, 0)),
        out_shape=jax.ShapeDtypeStruct((batch, heads, N_GATE_ROWS, s), F32),
        compiler_params=_cparams(("parallel",)),
        name="dn_gates",
    )(proj, a_log.reshape(heads, 1), dt_bias.reshape(heads, 1))


CONV_ROWS = 256


def _conv_silu(x_ref, w_ref, dst_ref, l2_scale):
    s, width = x_ref.shape
    w = w_ref[...]
    row8 = _iota2((8, width), 0)

    def block(i, _):
        r0 = pl.multiple_of(i * CONV_ROWS, CONV_ROWS)
        x = x_ref[pl.ds(r0, CONV_ROWS), :].astype(F32)
        p0 = pl.multiple_of(jnp.maximum(r0 - 16, 0), 16)
        prev = x_ref[pl.ds(p0, 16), :].astype(F32)[8:16]
        prev = prev * jnp.where(i > 0, 1.0, 0.0)
        acc = x * w[DN_CONV - 1:DN_CONV]
        for k in range(1, DN_CONV):
            xr = pltpu.roll(x, k, axis=0)
            pr = pltpu.roll(prev, k, axis=0)
            head = jnp.where(row8 < k, pr, xr[0:8])
            xr = jnp.concatenate([head, xr[8:]], axis=0)
            acc = acc + xr * w[DN_CONV - 1 - k:DN_CONV - k]
        y = acc * _sigmoid(acc)
        if l2_scale is not None:
            for c in range(width // HEAD_DIM):
                cs = slice(c * HEAD_DIM, (c + 1) * HEAD_DIM)
                yc = y[:, cs]
                ss = jnp.sum(yc * yc, axis=-1, keepdims=True)
                dst_ref[pl.ds(r0, CONV_ROWS), cs] = (yc * lax.rsqrt(ss + EPS)) * l2_scale
        else:
            dst_ref[pl.ds(r0, CONV_ROWS), :] = y
        return 0

    lax.fori_loop(0, s // CONV_ROWS, block, 0)


def _dn_kernel(q_ref, k_ref, v_ref, cwq_ref, cwk_ref, cwv_ref, gates_ref, og_ref, o_ref,
               qn_s, kn_s, vn_s, u0_s, wq_s, qkd_s, kdt_s, o_s):
    hq = pl.program_id(1)
    s = q_ref.shape[0]
    n_pairs = s // DN_PAIR

    _conv_silu(q_ref, cwq_ref, qn_s, HEAD_DIM ** -0.5)
    _conv_silu(k_ref, cwk_ref, kn_s, 1.0)
    _conv_silu(v_ref, cwv_ref, vn_s, None)

    ii = _iota2((DN_PAIR, DN_PAIR), 0)
    jj = _iota2((DN_PAIR, DN_PAIR), 1)
    eye = ii == jj
    incl = ((ii < DN_CHUNK) == (jj < DN_CHUNK)) & (ii >= jj)

    def gate_row(r, e, r0):
        return gates_ref[2 * hq + e, pl.ds(r, 1), pl.ds(r0, DN_PAIR)]

    def col(x):
        return jnp.broadcast_to(x, (DN_PAIR, DN_PAIR)).T

    def mm(a, b):
        return _dot(a.astype(BF16), b.astype(BF16))

    def local(p, _):
        r0 = pl.multiple_of(p * DN_PAIR, DN_PAIR)
        q = qn_s[pl.ds(r0, DN_PAIR), :]
        k = kn_s[pl.ds(r0, DN_PAIR), :]
        qk2 = _dot_nt(jnp.concatenate([q, k], axis=0).astype(BF16), k.astype(BF16))
        qk = qk2[:DN_PAIR]
        kk = qk2[DN_PAIR:]
        for e in range(2):
            gc_r = gate_row(1, e, r0)
            beta_c = col(gate_row(0, e, r0))
            gc_c = col(gc_r)
            gl_c = col(gate_row(2, e, r0))
            decay = jnp.exp(jnp.where(incl, gc_c - gc_r, NEG_BIG))
            a = jnp.where(eye, 0.0, beta_c * kk * decay)
            x = jnp.where(eye, 1.0, -a)
            pw = mm(a, a)
            for _ in range(4):
                x = x + mm(x, pw)
                pw = mm(pw, pw)
            x = x + mm(x, pw)
            ep_c = jnp.exp(gc_c)
            v = vn_s[pl.ds(r0, DN_PAIR), e * HEAD_DIM:(e + 1) * HEAD_DIM]
            rhs = jnp.concatenate([v * beta_c, k * (beta_c * ep_c)], axis=1)
            sol = mm(x, rhs)
            u0 = sol[:, :HEAD_DIM]
            w = sol[:, HEAD_DIM:]
            qd = q * ep_c
            kdt = (k * jnp.exp(gl_c)).T
            qkd = qk * decay
            for cc in range(2):
                c = 2 * p + cc
                rs = slice(cc * DN_CHUNK, (cc + 1) * DN_CHUNK)
                u0_s[e, c] = u0[rs]
                wq_s[e, c] = jnp.concatenate([w[rs], qd[rs]], axis=0).astype(BF16)
                qkd_s[e, c] = qkd[rs, rs].astype(BF16)
                kdt_s[e, c] = kdt[:, rs].astype(BF16)
        return 0

    lax.fori_loop(0, n_pairs, local, 0)

    def recur(p, states):
        r0 = pl.multiple_of(p * DN_PAIR, DN_PAIR)
        states = list(states)
        for cc in range(2):
            c = 2 * p + cc
            for e in range(2):
                st = states[e]
                r = _dot(wq_s[e, c], st.astype(BF16))
                u = (u0_s[e, c] - r[:DN_CHUNK]).astype(BF16)
                o = r[DN_CHUNK:] + _dot(qkd_s[e, c], u)
                row0 = pl.multiple_of(c * DN_CHUNK, DN_CHUNK)
                o_s[pl.ds(row0, DN_CHUNK), e * HEAD_DIM:(e + 1) * HEAD_DIM] = o
                states[e] = st * gate_row(3 + cc, e, r0) + _dot(kdt_s[e, c], u)
        return tuple(states)

    zero = jnp.zeros((HEAD_DIM, HEAD_DIM), F32)
    lax.fori_loop(0, n_pairs, recur, (zero, zero))

    for e in range(2):
        cs = slice(e * HEAD_DIM, (e + 1) * HEAD_DIM)
        o = o_s[:, cs]
        ms = jnp.mean(o * o, axis=-1, keepdims=True)
        o_ref[:, cs] = ((o * lax.rsqrt(ms + EPS)) * og_ref[...]).astype(o_ref.dtype)


def _deltanet(proj, conv_w, gates, out_g, batch, q_off, k_off, v_off):
    m = proj.shape[0]
    s = m // batch
    heads = gates.shape[1]
    qk_heads = heads // 2
    n_chunks = s // DN_CHUNK
    vw = 2 * HEAD_DIM
    qb, kb, vb = q_off // HEAD_DIM, k_off // HEAD_DIM, v_off // vw
    cv = 2 * qk_heads * HEAD_DIM // vw
    return pl.pallas_call(
        _dn_kernel,
        grid=(batch, qk_heads),
        in_specs=[
            pl.BlockSpec((s, HEAD_DIM), lambda b, h: (b, qb + h)),
            pl.BlockSpec((s, HEAD_DIM), lambda b, h: (b, kb + h)),
            pl.BlockSpec((s, vw), lambda b, h: (b, vb + h)),
            pl.BlockSpec((DN_CONV, HEAD_DIM), lambda b, h: (0, h)),
            pl.BlockSpec((DN_CONV, HEAD_DIM), lambda b, h: (0, qk_heads + h)),
            pl.BlockSpec((DN_CONV, vw), lambda b, h: (0, cv + h)),
            pl.BlockSpec((None, heads, N_GATE_ROWS, s), lambda b, h: (b, 0, 0, 0)),
            pl.BlockSpec((1, HEAD_DIM), lambda b, h: (0, 0)),
        ],
        out_specs=pl.BlockSpec((s, vw), lambda b, h: (b, h)),
        out_shape=jax.ShapeDtypeStruct((m, heads * HEAD_DIM), BF16),
        scratch_shapes=[
            pltpu.VMEM((s, HEAD_DIM), F32),
            pltpu.VMEM((s, HEAD_DIM), F32),
            pltpu.VMEM((s, vw), F32),
            pltpu.VMEM((2, n_chunks, DN_CHUNK, HEAD_DIM), F32),
            pltpu.VMEM((2, n_chunks, 2 * DN_CHUNK, HEAD_DIM), BF16),
            pltpu.VMEM((2, n_chunks, DN_CHUNK, DN_CHUNK), BF16),
            pltpu.VMEM((2, n_chunks, HEAD_DIM, DN_CHUNK), BF16),
            pltpu.VMEM((s, vw), F32),
        ],
        compiler_params=_cparams(("parallel", "arbitrary")),
        name="gated_deltanet",
    )(proj, proj, proj, conv_w, conv_w, conv_w, gates, out_g.reshape(1, HEAD_DIM))


SB_TILE = 256


def _sb_kernel(q_ref, k_ref, v_ref, qg_ref, kg_ref, o_ref, kn_s, acc_s, carry_s):
    qi = pl.program_id(2)
    t = SB_TILE

    @pl.when(qi == 0)
    def _():
        k = k_ref[...].astype(F32)
        ms = jnp.mean(k * k, axis=-1, keepdims=True)
        kn_s[...] = ((k * lax.rsqrt(ms + EPS)) * kg_ref[...]).astype(BF16)

    q = q_ref[...].astype(F32)
    ms = jnp.mean(q * q, axis=-1, keepdims=True)
    qn = ((q * lax.rsqrt(ms + EPS)) * qg_ref[...] * (HEAD_DIM ** -0.5)).astype(BF16)

    ii = _iota2((t, t), 0)
    jj = _iota2((t, t), 1)
    upper = jnp.where(ii > jj, 1.0, 0.0).astype(BF16)
    causal = jj < ii

    def tile(j, diagonal):
        k0 = pl.multiple_of(j * t, t)
        z = _dot_nt(qn, kn_s[pl.ds(k0, t), :])
        log_beta = jnp.minimum(z, 0.0) - jnp.log(1.0 + jnp.exp(-jnp.abs(z)))
        log_rest = log_beta - z
        if diagonal:
            log_rest = jnp.where(causal, log_rest, 0.0)
        hi = log_rest.astype(BF16)
        lo = (log_rest - hi.astype(F32)).astype(BF16)
        later = _dot(hi, upper) + _dot(lo, upper) + carry_s[...]
        w = jnp.exp(log_beta + later)
        if diagonal:
            w = jnp.where(causal, w, 0.0)
        acc_s[...] += _dot(w.astype(BF16), v_ref[pl.ds(k0, t), :])
        carry_s[...] += jnp.sum(log_rest, axis=-1, keepdims=True)

    acc_s[...] = jnp.zeros_like(acc_s)
    carry_s[...] = jnp.zeros_like(carry_s)
    tile(qi, True)

    def body(n, _):
        tile(qi - 1 - n, False)
        return 0

    lax.fori_loop(0, qi, body, 0)
    o_ref[...] = acc_s[...].astype(o_ref.dtype)


def _stickbreaking(proj, q_g, k_g, batch, heads, q_off, k_off, v_off):
    m = proj.shape[0]
    s = m // batch
    t = SB_TILE
    nq = s // t
    qb, kb, vb = q_off // HEAD_DIM, k_off // HEAD_DIM, v_off // HEAD_DIM
    return pl.pallas_call(
        _sb_kernel,
        grid=(batch, heads, nq),
        in_specs=[
            pl.BlockSpec((t, HEAD_DIM), lambda b, h, i: (b * nq + i, qb + h)),
            pl.BlockSpec((s, HEAD_DIM), lambda b, h, i: (b, kb + h)),
            pl.BlockSpec((s, HEAD_DIM), lambda b, h, i: (b, vb + h)),
            pl.BlockSpec((1, HEAD_DIM), lambda b, h, i: (0, 0)),
            pl.BlockSpec((1, HEAD_DIM), lambda b, h, i: (0, 0)),
        ],
        out_specs=pl.BlockSpec((t, HEAD_DIM), lambda b, h, i: (b * nq + i, h)),
        out_shape=jax.ShapeDtypeStruct((m, heads * HEAD_DIM), BF16),
        scratch_shapes=[
            pltpu.VMEM((s, HEAD_DIM), BF16),
            pltpu.VMEM((t, HEAD_DIM), F32),
            pltpu.VMEM((t, 1), F32),
        ],
        compiler_params=_cparams(("parallel", "parallel", "arbitrary")),
        name="stickbreaking_attn",
    )(proj, proj, proj, q_g.reshape(1, HEAD_DIM), k_g.reshape(1, HEAD_DIM))


def _regroup_in_weight(w, mix_cols, inner, pad_to):
    d = w.shape[0]
    parts = [w[:, mix_cols + XA_WIDTH:], w[:, mix_cols:mix_cols + XA_WIDTH], w[:, :mix_cols]]
    used = inner + XA_WIDTH + mix_cols
    if pad_to > used:
        parts.append(jnp.zeros((d, pad_to - used), w.dtype))
    return jnp.concatenate(parts, axis=1).astype(BF16)


def kernel(x, mem, norm_g, mem_norm_g, mem_w_kv, xa_q_norm_g, xa_k_norm_g, w_out,
           dn_w_in, dn_conv_w, dn_a_log, dn_dt_bias, dn_out_norm_g,
           sb_w_in, sb_q_norm_g, sb_k_norm_g):
    batch, seq, d = x.shape
    depth = norm_g.shape[0]
    inner = w_out.shape[1]
    mix_width = inner - XA_WIDTH
    heads = mix_width // HEAD_DIM
    m = batch * seq
    xq_off = inner
    mix_off = inner + XA_WIDTH

    dn_qk_width = (heads // 2) * HEAD_DIM
    dn_mix_cols = 2 * dn_qk_width + mix_width + 2 * heads
    dn_ab_off = mix_off + 2 * dn_qk_width + mix_width
    dn_cols = -(-(dn_ab_off + 2 * LANES) // 1280) * 1280
    sb_mix_cols = 3 * mix_width

    kv = _memkv(mem.reshape(-1, d), mem_norm_g, mem_w_kv.astype(BF16), xa_k_norm_g)
    w_out_b = w_out.astype(BF16)

    x2 = x.reshape(m, d)
    for i in range(depth):
        j = i // 2
        if i % 2 == 0:
            w_in = _regroup_in_weight(dn_w_in[j], dn_mix_cols, inner, dn_cols)
            proj = _inproj(x2, norm_g[i], w_in, tm=512, tn=1280)
            gates = _dn_gates(proj, dn_a_log[j], dn_dt_bias[j], batch, dn_ab_off)
            mix = _deltanet(proj, dn_conv_w[j], gates, dn_out_norm_g[j], batch,
                            mix_off, mix_off + dn_qk_width, mix_off + 2 * dn_qk_width)
        else:
            w_in = _regroup_in_weight(sb_w_in[j], sb_mix_cols, inner, 0)
            proj = _inproj(x2, norm_g[i], w_in, tm=512, tn=1792)
            mix = _stickbreaking(proj, sb_q_norm_g[j], sb_k_norm_g[j], batch, heads,
                                 mix_off, mix_off + mix_width, mix_off + 2 * mix_width)
        xa = _xattn(proj, kv, xa_q_norm_g[i], i, batch, xq_off)
        x2 = _outproj(proj, mix, xa, x2, w_out_b[i])
    return x2.reshape(batch, seq, d)
```

```python
import functools

import jax
import jax.numpy as jnp
from jax import lax
from jax.experimental import pallas as pl
from jax.experimental.pallas import tpu as pltpu

F32 = jnp.float32
BF16 = jnp.bfloat16
EPS = 1e-6

XA_HEADS = 4
XA_DIM = 256
XA_WIDTH = XA_HEADS * XA_DIM
HEAD_DIM = 128
DN_CONV = 4
DN_CHUNK = 64
DN_PAIR = 2 * DN_CHUNK
N_GATE_ROWS = 8

LANES = 128
VMEM_BYTES_V7X = 64 * 1024 * 1024
VMEM_LIMIT = VMEM_BYTES_V7X - 8 * 1024 * 1024

NEG_BIG = -1e30


def _cparams(sem):
    return pltpu.CompilerParams(dimension_semantics=sem, vmem_limit_bytes=VMEM_LIMIT)


def _dot(a, b):
    return jnp.dot(a, b, preferred_element_type=F32)


def _dot_nt(a, b):
    return lax.dot_general(a, b, (((1,), (1,)), ((), ())), preferred_element_type=F32)


def _split3(x):
    hi = x.astype(BF16)
    r = x - hi.astype(F32)
    mid = r.astype(BF16)
    lo = (r - mid.astype(F32)).astype(BF16)
    return hi, mid, lo


def _sigmoid(x):
    return 1.0 / (1.0 + jnp.exp(-x))


def _softplus(x):
    return jnp.maximum(x, 0.0) + jnp.log(1.0 + jnp.exp(-jnp.abs(x)))


def _iota2(shape, dim):
    return lax.broadcasted_iota(jnp.int32, shape, dim)


def _emit_staggered(items, skew):
    live = list(enumerate(items))
    stage = {n: 0 for n, _ in live}
    t = 0
    while live:
        due = [(n, g) for n, g in live if n + stage[n] * skew <= t]
        for n, g in sorted(due, key=lambda ng: -stage[ng[0]]):
            try:
                next(g)
                stage[n] += 1
            except StopIteration:
                live.remove((n, g))
        t += 1


def _inproj_kernel(x_ref, g_ref, w_ref, o_ref, h_ref):
    @pl.when(pl.program_id(1) == 0)
    def _():
        x = x_ref[...]
        ms = jnp.mean(x * x, axis=-1, keepdims=True)
        h_ref[...] = ((x * lax.rsqrt(ms + EPS)) * g_ref[...]).astype(BF16)

    o_ref[...] = _dot(h_ref[...], w_ref[...]).astype(o_ref.dtype)


def _inproj(x2, g, w, tm, tn):
    m, d = x2.shape
    p = w.shape[1]
    return pl.pallas_call(
        _inproj_kernel,
        grid=(m // tm, p // tn),
        in_specs=[
            pl.BlockSpec((tm, d), lambda i, j: (i, 0)),
            pl.BlockSpec((1, d), lambda i, j: (0, 0)),
            pl.BlockSpec((d, tn), lambda i, j: (0, j)),
        ],
        out_specs=pl.BlockSpec((tm, tn), lambda i, j: (i, j)),
        out_shape=jax.ShapeDtypeStruct((m, p), BF16),
        scratch_shapes=[pltpu.VMEM((tm, d), BF16)],
        compiler_params=_cparams(("parallel", "arbitrary")),
        name="rmsnorm_inproj",
    )(x2, g.reshape(1, d), w)


def _memkv_kernel(mem_ref, g_ref, w_ref, kg_ref, o_ref, *, k_tiles):
    x = mem_ref[...]
    ms = jnp.mean(x * x, axis=-1, keepdims=True)
    mn = ((x * lax.rsqrt(ms + EPS)) * g_ref[...]).astype(BF16)
    kv = _dot(mn, w_ref[0])
    is_key = pl.program_id(1) < k_tiles

    @pl.when(is_key)
    def _():
        for c in range(kv.shape[1] // XA_DIM):
            kh = kv[:, c * XA_DIM:(c + 1) * XA_DIM]
            hs = jnp.mean(kh * kh, axis=-1, keepdims=True)
            o_ref[0, :, c * XA_DIM:(c + 1) * XA_DIM] = (
                (kh * lax.rsqrt(hs + EPS)) * kg_ref[0]).astype(o_ref.dtype)

    @pl.when(jnp.logical_not(is_key))
    def _():
        o_ref[0] = kv.astype(o_ref.dtype)


def _memkv(mem2, mem_g, w_kv, k_g, tn=512):
    depth, d, e = w_kv.shape
    mm = mem2.shape[0]
    return pl.pallas_call(
        functools.partial(_memkv_kernel, k_tiles=XA_WIDTH // tn),
        grid=(depth, e // tn),
        in_specs=[
            pl.BlockSpec((mm, d), lambda l, j: (0, 0)),
            pl.BlockSpec((1, d), lambda l, j: (0, 0)),
            pl.BlockSpec((1, d, tn), lambda l, j: (l, 0, j)),
            pl.BlockSpec((1, 1, XA_DIM), lambda l, j: (l, 0, 0)),
        ],
        out_specs=pl.BlockSpec((1, mm, tn), lambda l, j: (l, 0, j)),
        out_shape=jax.ShapeDtypeStruct((depth, mm, e), BF16),
        compiler_params=_cparams(("parallel", "parallel")),
        name="mem_kv",
    )(mem2, mem_g.reshape(1, d), w_kv, k_g.reshape(depth, 1, XA_DIM))


def _xattn_kernel(q_ref, qg_ref, k_ref, v_ref, o_ref):
    for h in range(XA_HEADS):
        cs = slice(h * XA_DIM, (h + 1) * XA_DIM)
        q = q_ref[:, cs].astype(F32)
        ms = jnp.mean(q * q, axis=-1, keepdims=True)
        qn = ((q * lax.rsqrt(ms + EPS)) * qg_ref[...]).astype(BF16)
        s = _dot_nt(qn, k_ref[0, :, cs]) * (XA_DIM ** -0.5)
        e = jnp.exp(s - jnp.max(s, axis=-1, keepdims=True))
        p = e / jnp.sum(e, axis=-1, keepdims=True)
        o_ref[:, cs] = _dot(p.astype(BF16), v_ref[0, :, cs]).astype(o_ref.dtype)


def _xattn(proj, kv, q_g, layer, batch, xq_off, tm=512):
    m = proj.shape[0]
    n_mem = kv.shape[1] // batch
    nt = m // batch // tm
    qb = xq_off // XA_WIDTH
    return pl.pallas_call(
        _xattn_kernel,
        grid=(batch, nt),
        in_specs=[
            pl.BlockSpec((tm, XA_WIDTH), lambda b, i: (b * nt + i, qb)),
            pl.BlockSpec((1, XA_DIM), lambda b, i: (0, 0)),
            pl.BlockSpec((1, n_mem, XA_WIDTH), lambda b, i: (layer, b, 0)),
            pl.BlockSpec((1, n_mem, XA_WIDTH), lambda b, i: (layer, b, 1)),
        ],
        out_specs=pl.BlockSpec((tm, XA_WIDTH), lambda b, i: (b * nt + i, 0)),
        out_shape=jax.ShapeDtypeStruct((m, XA_WIDTH), BF16),
        compiler_params=_cparams(("parallel", "parallel")),
        name="mem_xattn",
    )(proj, q_g.reshape(1, XA_DIM), kv, kv)


def _outproj_kernel(z0_ref, z1_ref, z2_ref, z3_ref, mix_ref, xa_ref, x_ref, w_ref, o_ref, y_s):
    zw = z0_ref.shape[1]
    n_mix = mix_ref.shape[1] // zw
    for c, z_ref in enumerate((z0_ref, z1_ref, z2_ref, z3_ref)):
        z = z_ref[...].astype(F32)
        if c < n_mix:
            br = mix_ref[:, c * zw:(c + 1) * zw]
        else:
            br = xa_ref[:, (c - n_mix) * zw:(c - n_mix + 1) * zw]
        y_s[:, c * zw:(c + 1) * zw] = (br.astype(F32) * (z * _sigmoid(z))).astype(BF16)
    o_ref[...] = x_ref[...] + _dot(y_s[...], w_ref[...])


def _outproj(proj, z_off, mix, xa, x2, w, tm=256):
    m, d = x2.shape
    inner = w.shape[0]
    mw = mix.shape[1]
    zw = inner // 4
    zb = z_off // zw
    z_spec = lambda c: pl.BlockSpec((tm, zw), lambda i: (i, zb + c))
    return pl.pallas_call(
        _outproj_kernel,
        grid=(m // tm,),
        in_specs=[
            z_spec(0), z_spec(1), z_spec(2), z_spec(3),
            pl.BlockSpec((tm, mw), lambda i: (i, 0)),
            pl.BlockSpec((tm, inner - mw), lambda i: (i, 0)),
            pl.BlockSpec((tm, d), lambda i: (i, 0)),
            pl.BlockSpec((inner, d), lambda i: (0, 0), pipeline_mode=pl.Buffered(1)),
        ],
        out_specs=pl.BlockSpec((tm, d), lambda i: (i, 0)),
        out_shape=jax.ShapeDtypeStruct((m, d), F32),
        scratch_shapes=[pltpu.VMEM((tm, inner), BF16)],
        compiler_params=_cparams(("parallel",)),
        name="gate_outproj",
    )(proj, proj, proj, proj, mix, xa, x2, w)


def _dn_gates_kernel(ab_ref, alog_ref, dtb_ref, o_ref, *, heads):
    s = ab_ref.shape[0]
    t = ab_ref[:, :LANES].astype(F32).T
    a = t[0:heads]
    b = t[heads:2 * heads]
    g = -jnp.exp(alog_ref[...]) * _softplus(a + dtb_ref[...])
    beta = _sigmoid(b)

    ii = _iota2((DN_PAIR, DN_PAIR), 0)
    jj = _iota2((DN_PAIR, DN_PAIR), 1)
    lo_i = ii < DN_CHUNK
    lo_j = jj < DN_CHUNK
    same = lo_i == lo_j
    one = lambda m: jnp.where(m, 1.0, 0.0).astype(BF16)
    rhs = jnp.concatenate(
        [one(same & (ii <= jj)), one(same), one(lo_i), one(jnp.logical_not(lo_i))], axis=1)

    for p in range(s // DN_PAIR):
        ls = slice(p * DN_PAIR, (p + 1) * DN_PAIR)
        hi, mid, lo = _split3(g[:, ls])
        r = _dot(hi, rhs) + _dot(mid, rhs) + _dot(lo, rhs)
        gc = r[:, 0:DN_PAIR]
        rows = (beta[:, ls], gc, r[:, DN_PAIR:2 * DN_PAIR] - gc,
                jnp.exp(r[:, 2 * DN_PAIR:3 * DN_PAIR]), jnp.exp(r[:, 3 * DN_PAIR:4 * DN_PAIR]))
        for n, val in enumerate(rows):
            o_ref[:, n, ls] = val
        for n in range(len(rows), N_GATE_ROWS):
            o_ref[:, n, ls] = jnp.zeros_like(gc)


def _dn_gates(proj, a_log, dt_bias, batch, ab_off):
    m = proj.shape[0]
    s = m // batch
    heads = a_log.shape[0]
    blk = 2 * LANES
    return pl.pallas_call(
        functools.partial(_dn_gates_kernel, heads=heads),
        grid=(batch,),
        in_specs=[
            pl.BlockSpec((s, blk), lambda b: (b, ab_off // blk)),
            pl.BlockSpec((heads, 1), lambda b: (0, 0)),
            pl.BlockSpec((heads, 1), lambda b: (0, 0)),
        ],
        out_specs=pl.BlockSpec((None, heads, N_GATE_ROWS, s), lambda b: (b, 0, 0, 0)),
        out_shape=jax.ShapeDtypeStruct((batch, heads, N_GATE_ROWS, s), F32),
        compiler_params=_cparams(("parallel",)),
        name="dn_gates",
    )(proj, a_log.reshape(heads, 1), dt_bias.reshape(heads, 1))


CONV_ROWS = 256


def _conv_silu(x_ref, w_ref, dst_ref, l2_scale):
    s, width = x_ref.shape
    w = w_ref[...]
    row8 = _iota2((8, width), 0)

    def block(i, _):
        r0 = pl.multiple_of(i * CONV_ROWS, CONV_ROWS)
        x = x_ref[pl.ds(r0, CONV_ROWS), :].astype(F32)
        p0 = pl.multiple_of(jnp.maximum(r0 - 16, 0), 16)
        prev = x_ref[pl.ds(p0, 16), :].astype(F32)[8:16]
        prev = prev * jnp.where(i > 0, 1.0, 0.0)
        acc = x * w[DN_CONV - 1:DN_CONV]
        for k in range(1, DN_CONV):
            xr = pltpu.roll(x, k, axis=0)
            pr = pltpu.roll(prev, k, axis=0)
            head = jnp.where(row8 < k, pr, xr[0:8])
            xr = jnp.concatenate([head, xr[8:]], axis=0)
            acc = acc + xr * w[DN_CONV - 1 - k:DN_CONV - k]
        y = acc * _sigmoid(acc)
        if l2_scale is not None:
            for c in range(width // HEAD_DIM):
                cs = slice(c * HEAD_DIM, (c + 1) * HEAD_DIM)
                yc = y[:, cs]
                ss = jnp.sum(yc * yc, axis=-1, keepdims=True)
                dst_ref[pl.ds(r0, CONV_ROWS), cs] = (yc * lax.rsqrt(ss + EPS)) * l2_scale
        else:
            dst_ref[pl.ds(r0, CONV_ROWS), :] = y
        return 0

    lax.fori_loop(0, s // CONV_ROWS, block, 0)


DN_QK_PER_STEP = 2
DN_LOCAL_PAIRS = 2


def _dn_kernel(q_ref, k_ref, v_ref, cwq_ref, cwk_ref, cwv_ref, gates_ref, og_ref, o_ref,
               qn_s, kn_s, vn_s, u0_s, wq_s, qkd_s, kdt_s, st_s):
    n_vh = 2 * DN_QK_PER_STEP
    head0 = n_vh * pl.program_id(1)
    s = q_ref.shape[0]
    n_pairs = s // DN_PAIR

    _conv_silu(q_ref, cwq_ref, qn_s, HEAD_DIM ** -0.5)
    _conv_silu(k_ref, cwk_ref, kn_s, 1.0)
    _conv_silu(v_ref, cwv_ref, vn_s, None)

    ii = _iota2((DN_PAIR, DN_PAIR), 0)
    jj = _iota2((DN_PAIR, DN_PAIR), 1)
    eye = ii == jj
    incl = ((ii < DN_CHUNK) == (jj < DN_CHUNK)) & (ii >= jj)

    def gate_row(r, vh, r0):
        return gates_ref[head0 + vh, pl.ds(r, 1), pl.ds(r0, DN_PAIR)]

    def col(x):
        return jnp.broadcast_to(x, (DN_PAIR, DN_PAIR)).T

    def mm(a, b):
        return _dot(a.astype(BF16), b.astype(BF16))

    def hs(n):
        return slice(n * HEAD_DIM, (n + 1) * HEAD_DIM)

    def local_chain(p, qh, e, q, k, qk, kk):
        r0 = pl.multiple_of(p * DN_PAIR, DN_PAIR)
        vh = 2 * qh + e
        gc_r = gate_row(1, vh, r0)
        beta_c = col(gate_row(0, vh, r0))
        gc_c = col(gc_r)
        gl_c = col(gate_row(2, vh, r0))
        decay = jnp.exp(jnp.where(incl, gc_c - gc_r, NEG_BIG))
        a = jnp.where(eye, 0.0, beta_c * kk * decay)
        x = jnp.where(eye, 1.0, -a)
        pw = mm(a, a)
        yield
        for _ in range(4):
            x, pw = x + mm(x, pw), mm(pw, pw)
            yield
        x = x + mm(x, pw)
        ep_c = jnp.exp(gc_c)
        v = vn_s[pl.ds(r0, DN_PAIR), hs(vh)]
        rhs = jnp.concatenate([v * beta_c, k * (beta_c * ep_c)], axis=1)
        yield
        sol = mm(x, rhs)
        qd = q * ep_c
        kdt = (k * jnp.exp(gl_c)).T
        qkd = qk * decay
        yield
        u0 = sol[:, :HEAD_DIM]
        w = sol[:, HEAD_DIM:]
        for cc in range(2):
            c = 2 * p + cc
            rs = slice(cc * DN_CHUNK, (cc + 1) * DN_CHUNK)
            u0_s[vh, c] = u0[rs]
            wq_s[vh, c] = jnp.concatenate([w[rs], qd[rs]], axis=0).astype(BF16)
            qkd_s[vh, c] = qkd[rs, rs].astype(BF16)
            kdt_s[vh, c] = kdt[:, rs].astype(BF16)

    def local(it, _):
        chains = []
        for pp in range(DN_LOCAL_PAIRS):
            p = DN_LOCAL_PAIRS * it + pp
            r0 = pl.multiple_of(p * DN_PAIR, DN_PAIR)
            for qh in range(DN_QK_PER_STEP):
                q = qn_s[pl.ds(r0, DN_PAIR), hs(qh)]
                k = kn_s[pl.ds(r0, DN_PAIR), hs(qh)]
                qk2 = _dot_nt(jnp.concatenate([q, k], axis=0).astype(BF16), k.astype(BF16))
                for e in range(2):
                    chains.append(local_chain(p, qh, e, q, k, qk2[:DN_PAIR], qk2[DN_PAIR:]))
        _emit_staggered(chains, 0)
        return 0

    lax.fori_loop(0, n_pairs // DN_LOCAL_PAIRS, local, 0)

    og = og_ref[...]

    def recur_head(vh, p, cc):
        r0 = pl.multiple_of(p * DN_PAIR, DN_PAIR)
        c = 2 * p + cc
        row0 = pl.multiple_of(c * DN_CHUNK, DN_CHUNK)
        r = _dot(wq_s[vh, c], st_s[vh].astype(BF16))
        yield
        u = (u0_s[vh, c] - r[:DN_CHUNK]).astype(BF16)
        o = r[DN_CHUNK:] + _dot(qkd_s[vh, c], u)
        st_s[vh] = st_s[vh] * gate_row(3 + cc, vh, r0) + _dot(kdt_s[vh, c], u)
        yield
        ms = jnp.mean(o * o, axis=-1, keepdims=True)
        o_ref[pl.ds(row0, DN_CHUNK), hs(vh)] = ((o * lax.rsqrt(ms + EPS)) * og).astype(o_ref.dtype)

    def recur(p, _):
        for cc in range(2):
            _emit_staggered([recur_head(vh, p, cc) for vh in range(n_vh)], 0)
        return 0

    st_s[...] = jnp.zeros_like(st_s)
    lax.fori_loop(0, n_pairs, recur, 0)


def _deltanet(proj, conv_w, gates, out_g, batch, q_off, k_off, v_off):
    m = proj.shape[0]
    s = m // batch
    heads = gates.shape[1]
    n_vh = 2 * DN_QK_PER_STEP
    steps = heads // n_vh
    n_chunks = s // DN_CHUNK
    qw = DN_QK_PER_STEP * HEAD_DIM
    vw = n_vh * HEAD_DIM
    qb, kb, vb = q_off // qw, k_off // qw, v_off // vw
    ck = (heads // 2) * HEAD_DIM // qw
    cv = heads * HEAD_DIM // vw
    return pl.pallas_call(
        _dn_kernel,
        grid=(batch, steps),
        in_specs=[
            pl.BlockSpec((s, qw), lambda b, h: (b, qb + h)),
            pl.BlockSpec((s, qw), lambda b, h: (b, kb + h)),
            pl.BlockSpec((s, vw), lambda b, h: (b, vb + h)),
            pl.BlockSpec((DN_CONV, qw), lambda b, h: (0, h)),
            pl.BlockSpec((DN_CONV, qw), lambda b, h: (0, ck + h)),
            pl.BlockSpec((DN_CONV, vw), lambda b, h: (0, cv + h)),
            pl.BlockSpec((None, heads, N_GATE_ROWS, s), lambda b, h: (b, 0, 0, 0)),
            pl.BlockSpec((1, HEAD_DIM), lambda b, h: (0, 0)),
        ],
        out_specs=pl.BlockSpec((s, vw), lambda b, h: (b, h)),
        out_shape=jax.ShapeDtypeStruct((m, heads * HEAD_DIM), BF16),
        scratch_shapes=[
            pltpu.VMEM((s, qw), F32),
            pltpu.VMEM((s, qw), F32),
            pltpu.VMEM((s, vw), F32),
            pltpu.VMEM((n_vh, n_chunks, DN_CHUNK, HEAD_DIM), F32),
            pltpu.VMEM((n_vh, n_chunks, 2 * DN_CHUNK, HEAD_DIM), BF16),
            pltpu.VMEM((n_vh, n_chunks, DN_CHUNK, DN_CHUNK), BF16),
            pltpu.VMEM((n_vh, n_chunks, HEAD_DIM, DN_CHUNK), BF16),
            pltpu.VMEM((n_vh, HEAD_DIM, HEAD_DIM), F32),
        ],
        compiler_params=_cparams(("parallel", "arbitrary")),
        name="gated_deltanet",
    )(proj, proj, proj, conv_w, conv_w, conv_w, gates, out_g.reshape(1, HEAD_DIM))


SB_TILE = 256


SB_DONE = -110.0
SB_SKEW = 2


def _sb_kernel(q_ref, k_ref, v_ref, qg_ref, kg_ref, o_ref, qn_s, kn_s, acc_s, carry_s):
    s = q_ref.shape[0]
    t = SB_TILE
    nq = s // t

    def normalise(x_ref, g_ref, scale, dst):
        x = x_ref[...].astype(F32)
        ms = jnp.mean(x * x, axis=-1, keepdims=True)
        dst[...] = ((x * lax.rsqrt(ms + EPS)) * g_ref[...] * scale).astype(BF16)

    normalise(q_ref, qg_ref, HEAD_DIM ** -0.5, qn_s)
    normalise(k_ref, kg_ref, 1.0, kn_s)

    ii = _iota2((t, t), 0)
    jj = _iota2((t, t), 1)
    upper = jnp.where(ii > jj, 1.0, 0.0).astype(BF16)
    causal = jj < ii

    def tile(q0, k0, diagonal):
        z = _dot_nt(qn_s[pl.ds(q0, t), :], kn_s[pl.ds(k0, t), :])
        yield
        log_beta = jnp.minimum(z, 0.0) - jnp.log(1.0 + jnp.exp(-jnp.abs(z)))
        log_rest = log_beta - z
        if diagonal:
            log_rest = jnp.where(causal, log_rest, 0.0)
        hi = log_rest.astype(BF16)
        lo = (log_rest - hi.astype(F32)).astype(BF16)
        later = _dot(hi, upper) + _dot(lo, upper)
        row_sum = jnp.sum(log_rest, axis=-1, keepdims=True)
        yield
        if diagonal:
            w = jnp.where(causal, jnp.exp(log_beta + later), 0.0)
            acc_s[pl.ds(q0, t), :] = _dot(w.astype(BF16), v_ref[pl.ds(k0, t), :])
            carry_s[pl.ds(q0, t), :] = row_sum
        else:
            carry = carry_s[pl.ds(q0, t), :]
            w = jnp.exp(log_beta + (later + carry))
            acc_s[pl.ds(q0, t), :] += _dot(w.astype(BF16), v_ref[pl.ds(k0, t), :])
            carry_s[pl.ds(q0, t), :] = carry + row_sum

    first = [tile(i * t, i * t, True) for i in range(nq)]
    second = [tile(i * t, (i - 1) * t, False) for i in range(1, nq)]
    _emit_staggered(first + second, SB_SKEW)

    rows = _iota2((s, 1), 0)

    def sweep(d, _):
        live = jnp.max(jnp.where(rows >= d * t, carry_s[...], SB_DONE)) > SB_DONE

        @pl.when(live)
        def _():
            def body(i, _):
                for _ in tile(pl.multiple_of(i * t, t), pl.multiple_of((i - d) * t, t), False):
                    pass
                return 0

            lax.fori_loop(d, nq, body, 0)

        return 0

    lax.fori_loop(2, nq, sweep, 0)
    o_ref[...] = acc_s[...].astype(o_ref.dtype)


def _stickbreaking(proj, q_g, k_g, batch, heads, q_off, k_off, v_off):
    m = proj.shape[0]
    s = m // batch
    qb, kb, vb = q_off // HEAD_DIM, k_off // HEAD_DIM, v_off // HEAD_DIM
    return pl.pallas_call(
        _sb_kernel,
        grid=(batch, heads),
        in_specs=[
            pl.BlockSpec((s, HEAD_DIM), lambda b, h: (b, qb + h)),
            pl.BlockSpec((s, HEAD_DIM), lambda b, h: (b, kb + h)),
            pl.BlockSpec((s, HEAD_DIM), lambda b, h: (b, vb + h)),
            pl.BlockSpec((1, HEAD_DIM), lambda b, h: (0, 0)),
            pl.BlockSpec((1, HEAD_DIM), lambda b, h: (0, 0)),
        ],
        out_specs=pl.BlockSpec((s, HEAD_DIM), lambda b, h: (b, h)),
        out_shape=jax.ShapeDtypeStruct((m, heads * HEAD_DIM), BF16),
        scratch_shapes=[
            pltpu.VMEM((s, HEAD_DIM), BF16),
            pltpu.VMEM((s, HEAD_DIM), BF16),
            pltpu.VMEM((s, HEAD_DIM), F32),
            pltpu.VMEM((s, 1), F32),
        ],
        compiler_params=_cparams(("parallel", "parallel")),
        name="stickbreaking_attn",
    )(proj, proj, proj, q_g.reshape(1, HEAD_DIM), k_g.reshape(1, HEAD_DIM))


def kernel(x, mem, norm_g, mem_norm_g, mem_w_kv, xa_q_norm_g, xa_k_norm_g, w_out,
           dn_w_in, dn_conv_w, dn_a_log, dn_dt_bias, dn_out_norm_g,
           sb_w_in, sb_q_norm_g, sb_k_norm_g):
    batch, seq, d = x.shape
    depth = norm_g.shape[0]
    inner = w_out.shape[1]
    mix_width = inner - XA_WIDTH
    heads = mix_width // HEAD_DIM
    m = batch * seq
    tn_dn = 1280

    dn_qk_width = (heads // 2) * HEAD_DIM
    dn_ab_off = 2 * dn_qk_width + mix_width
    dn_mix_cols = dn_ab_off + 2 * heads
    dn_mix_pad = -(-(dn_ab_off + 2 * LANES) // tn_dn) * tn_dn
    sb_mix_cols = 3 * mix_width

    kv = _memkv(mem.reshape(-1, d), mem_norm_g, mem_w_kv.astype(BF16), xa_k_norm_g)
    w_out_b = w_out.astype(BF16)

    x2 = x.reshape(m, d)
    for i in range(depth):
        j = i // 2
        if i % 2 == 0:
            w_mix = jnp.pad(dn_w_in[j][:, :dn_mix_cols],
                            ((0, 0), (0, dn_mix_pad - dn_mix_cols))).astype(BF16)
            w_side = dn_w_in[j][:, dn_mix_cols:].astype(BF16)
            pmix = _inproj(x2, norm_g[i], w_mix, tm=512, tn=tn_dn)
            side = _inproj(x2, norm_g[i], w_side, tm=512, tn=tn_dn)
            xq_off = 0
            gates = _dn_gates(pmix, dn_a_log[j], dn_dt_bias[j], batch, dn_ab_off)
            mix = _deltanet(pmix, dn_conv_w[j], gates, dn_out_norm_g[j], batch,
                            0, dn_qk_width, 2 * dn_qk_width)
        else:
            side = _inproj(x2, norm_g[i], sb_w_in[j].astype(BF16), tm=512, tn=1792)
            xq_off = sb_mix_cols
            mix = _stickbreaking(side, sb_q_norm_g[j], sb_k_norm_g[j], batch, heads,
                                 0, mix_width, 2 * mix_width)
        xa = _xattn(side, kv, xa_q_norm_g[i], i, batch, xq_off)
        x2 = _outproj(side, xq_off + XA_WIDTH, mix, xa, x2, w_out_b[i])
    return x2.reshape(batch, seq, d)
```

```python
import functools

import jax
import jax.numpy as jnp
from jax import lax
from jax.experimental import pallas as pl
from jax.experimental.pallas import tpu as pltpu

F32 = jnp.float32
BF16 = jnp.bfloat16
EPS = 1e-6

XA_HEADS = 4
XA_DIM = 256
XA_WIDTH = XA_HEADS * XA_DIM
HEAD_DIM = 128
DN_CONV = 4
DN_CHUNK = 64
DN_PAIR = 2 * DN_CHUNK
N_GATE_ROWS = 8

LANES = 128
VMEM_BYTES_V7X = 64 * 1024 * 1024
VMEM_LIMIT = VMEM_BYTES_V7X - 8 * 1024 * 1024

NEG_BIG = -1e30


def _cparams(sem):
    return pltpu.CompilerParams(dimension_semantics=sem, vmem_limit_bytes=VMEM_LIMIT)


def _dot(a, b):
    return jnp.dot(a, b, preferred_element_type=F32)


def _dot_nt(a, b):
    return lax.dot_general(a, b, (((1,), (1,)), ((), ())), preferred_element_type=F32)


def _split3(x):
    hi = x.astype(BF16)
    r = x - hi.astype(F32)
    mid = r.astype(BF16)
    lo = (r - mid.astype(F32)).astype(BF16)
    return hi, mid, lo


def _sigmoid(x):
    return 1.0 / (1.0 + jnp.exp(-x))


def _softplus(x):
    return jnp.maximum(x, 0.0) + jnp.log(1.0 + jnp.exp(-jnp.abs(x)))


def _iota2(shape, dim):
    return lax.broadcasted_iota(jnp.int32, shape, dim)


def _emit_staggered(items, lag, skew=1):
    live = list(enumerate(items))
    stage = {n: 0 for n, _ in live}
    t = 0
    while live:
        due = [(n, g) for n, g in live if n * lag + stage[n] * skew <= t]
        for n, g in sorted(due, key=lambda ng: -stage[ng[0]]):
            try:
                next(g)
                stage[n] += 1
            except StopIteration:
                live.remove((n, g))
        t += 1


def _inproj_kernel(x_ref, g_ref, w_ref, o_ref, h_ref):
    @pl.when(pl.program_id(1) == 0)
    def _():
        x = x_ref[...]
        ms = jnp.mean(x * x, axis=-1, keepdims=True)
        h_ref[...] = ((x * lax.rsqrt(ms + EPS)) * g_ref[...]).astype(BF16)

    o_ref[...] = _dot(h_ref[...], w_ref[...]).astype(o_ref.dtype)


def _inproj(x2, g, w, tm, tn):
    m, d = x2.shape
    p = w.shape[1]
    return pl.pallas_call(
        _inproj_kernel,
        grid=(m // tm, p // tn),
        in_specs=[
            pl.BlockSpec((tm, d), lambda i, j: (i, 0)),
            pl.BlockSpec((1, d), lambda i, j: (0, 0)),
            pl.BlockSpec((d, tn), lambda i, j: (0, j)),
        ],
        out_specs=pl.BlockSpec((tm, tn), lambda i, j: (i, j)),
        out_shape=jax.ShapeDtypeStruct((m, p), BF16),
        scratch_shapes=[pltpu.VMEM((tm, d), BF16)],
        compiler_params=_cparams(("parallel", "arbitrary")),
        name="rmsnorm_inproj",
    )(x2, g.reshape(1, d), w)


def _memkv_kernel(mem_ref, g_ref, w_ref, kg_ref, o_ref, *, k_tiles):
    x = mem_ref[...]
    ms = jnp.mean(x * x, axis=-1, keepdims=True)
    mn = ((x * lax.rsqrt(ms + EPS)) * g_ref[...]).astype(BF16)
    kv = _dot(mn, w_ref[0])
    is_key = pl.program_id(1) < k_tiles

    @pl.when(is_key)
    def _():
        for c in range(kv.shape[1] // XA_DIM):
            kh = kv[:, c * XA_DIM:(c + 1) * XA_DIM]
            hs = jnp.mean(kh * kh, axis=-1, keepdims=True)
            o_ref[0, :, c * XA_DIM:(c + 1) * XA_DIM] = (
                (kh * lax.rsqrt(hs + EPS)) * kg_ref[0]).astype(o_ref.dtype)

    @pl.when(jnp.logical_not(is_key))
    def _():
        o_ref[0] = kv.astype(o_ref.dtype)


def _memkv(mem2, mem_g, w_kv, k_g, tn=512):
    depth, d, e = w_kv.shape
    mm = mem2.shape[0]
    return pl.pallas_call(
        functools.partial(_memkv_kernel, k_tiles=XA_WIDTH // tn),
        grid=(depth, e // tn),
        in_specs=[
            pl.BlockSpec((mm, d), lambda l, j: (0, 0)),
            pl.BlockSpec((1, d), lambda l, j: (0, 0)),
            pl.BlockSpec((1, d, tn), lambda l, j: (l, 0, j)),
            pl.BlockSpec((1, 1, XA_DIM), lambda l, j: (l, 0, 0)),
        ],
        out_specs=pl.BlockSpec((1, mm, tn), lambda l, j: (l, 0, j)),
        out_shape=jax.ShapeDtypeStruct((depth, mm, e), BF16),
        compiler_params=_cparams(("parallel", "parallel")),
        name="mem_kv",
    )(mem2, mem_g.reshape(1, d), w_kv, k_g.reshape(depth, 1, XA_DIM))


def _xattn_kernel(q_ref, qg_ref, k_ref, v_ref, o_ref):
    for h in range(XA_HEADS):
        cs = slice(h * XA_DIM, (h + 1) * XA_DIM)
        q = q_ref[:, cs].astype(F32)
        ms = jnp.mean(q * q, axis=-1, keepdims=True)
        qn = ((q * lax.rsqrt(ms + EPS)) * qg_ref[...]).astype(BF16)
        s = _dot_nt(qn, k_ref[0, :, cs]) * (XA_DIM ** -0.5)
        e = jnp.exp(s - jnp.max(s, axis=-1, keepdims=True))
        p = e / jnp.sum(e, axis=-1, keepdims=True)
        o_ref[:, cs] = _dot(p.astype(BF16), v_ref[0, :, cs]).astype(o_ref.dtype)


def _xattn(proj, kv, q_g, layer, batch, xq_off, tm=512):
    m = proj.shape[0]
    n_mem = kv.shape[1] // batch
    nt = m // batch // tm
    qb = xq_off // XA_WIDTH
    return pl.pallas_call(
        _xattn_kernel,
        grid=(batch, nt),
        in_specs=[
            pl.BlockSpec((tm, XA_WIDTH), lambda b, i: (b * nt + i, qb)),
            pl.BlockSpec((1, XA_DIM), lambda b, i: (0, 0)),
            pl.BlockSpec((1, n_mem, XA_WIDTH), lambda b, i: (layer, b, 0)),
            pl.BlockSpec((1, n_mem, XA_WIDTH), lambda b, i: (layer, b, 1)),
        ],
        out_specs=pl.BlockSpec((tm, XA_WIDTH), lambda b, i: (b * nt + i, 0)),
        out_shape=jax.ShapeDtypeStruct((m, XA_WIDTH), BF16),
        compiler_params=_cparams(("parallel", "parallel")),
        name="mem_xattn",
    )(proj, q_g.reshape(1, XA_DIM), kv, kv)


def _outproj_kernel(z0_ref, z1_ref, z2_ref, z3_ref, mix_ref, xa_ref, x_ref, w_ref, o_ref, y_s):
    zw = z0_ref.shape[1]
    n_mix = mix_ref.shape[1] // zw
    for c, z_ref in enumerate((z0_ref, z1_ref, z2_ref, z3_ref)):
        z = z_ref[...].astype(F32)
        if c < n_mix:
            br = mix_ref[:, c * zw:(c + 1) * zw]
        else:
            br = xa_ref[:, (c - n_mix) * zw:(c - n_mix + 1) * zw]
        y_s[:, c * zw:(c + 1) * zw] = (br.astype(F32) * (z * _sigmoid(z))).astype(BF16)
    o_ref[...] = x_ref[...] + _dot(y_s[...], w_ref[...])


def _outproj(proj, z_off, mix, xa, x2, w, layer, tm=256):
    m, d = x2.shape
    inner = w.shape[1]
    mw = mix.shape[1]
    zw = inner // 4
    zb = z_off // zw
    z_spec = lambda c: pl.BlockSpec((tm, zw), lambda i: (i, zb + c))
    return pl.pallas_call(
        _outproj_kernel,
        grid=(m // tm,),
        in_specs=[
            z_spec(0), z_spec(1), z_spec(2), z_spec(3),
            pl.BlockSpec((tm, mw), lambda i: (i, 0)),
            pl.BlockSpec((tm, inner - mw), lambda i: (i, 0)),
            pl.BlockSpec((tm, d), lambda i: (i, 0)),
            pl.BlockSpec((None, inner, d), lambda i: (layer, 0, 0), pipeline_mode=pl.Buffered(1)),
        ],
        out_specs=pl.BlockSpec((tm, d), lambda i: (i, 0)),
        out_shape=jax.ShapeDtypeStruct((m, d), F32),
        scratch_shapes=[pltpu.VMEM((tm, inner), BF16)],
        compiler_params=_cparams(("parallel",)),
        name="gate_outproj",
    )(proj, proj, proj, proj, mix, xa, x2, w)


def _dn_gates_kernel(ab_ref, alog_ref, dtb_ref, o_ref, *, heads):
    s = ab_ref.shape[0]
    t = ab_ref[:, :LANES].astype(F32).T
    a = t[0:heads]
    b = t[heads:2 * heads]
    g = -jnp.exp(alog_ref[...]) * _softplus(a + dtb_ref[...])
    beta = _sigmoid(b)

    ii = _iota2((DN_PAIR, DN_PAIR), 0)
    jj = _iota2((DN_PAIR, DN_PAIR), 1)
    lo_i = ii < DN_CHUNK
    lo_j = jj < DN_CHUNK
    same = lo_i == lo_j
    one = lambda m: jnp.where(m, 1.0, 0.0).astype(BF16)
    rhs = jnp.concatenate(
        [one(same & (ii <= jj)), one(same), one(lo_i), one(jnp.logical_not(lo_i))], axis=1)

    for p in range(s // DN_PAIR):
        ls = slice(p * DN_PAIR, (p + 1) * DN_PAIR)
        hi, mid, lo = _split3(g[:, ls])
        r = _dot(hi, rhs) + _dot(mid, rhs) + _dot(lo, rhs)
        gc = r[:, 0:DN_PAIR]
        rows = (beta[:, ls], gc, r[:, DN_PAIR:2 * DN_PAIR] - gc,
                jnp.exp(r[:, 2 * DN_PAIR:3 * DN_PAIR]), jnp.exp(r[:, 3 * DN_PAIR:4 * DN_PAIR]))
        for n, val in enumerate(rows):
            o_ref[:, n, ls] = val
        for n in range(len(rows), N_GATE_ROWS):
            o_ref[:, n, ls] = jnp.zeros_like(gc)


def _dn_gates(proj, a_log, dt_bias, batch, ab_off):
    m = proj.shape[0]
    s = m // batch
    heads = a_log.shape[0]
    blk = 2 * LANES
    return pl.pallas_call(
        functools.partial(_dn_gates_kernel, heads=heads),
        grid=(batch,),
        in_specs=[
            pl.BlockSpec((s, blk), lambda b: (b, ab_off // blk)),
            pl.BlockSpec((heads, 1), lambda b: (0, 0)),
            pl.BlockSpec((heads, 1), lambda b: (0, 0)),
        ],
        out_specs=pl.BlockSpec((None, heads, N_GATE_ROWS, s), lambda b: (b, 0, 0, 0)),
        out_shape=jax.ShapeDtypeStruct((batch, heads, N_GATE_ROWS, s), F32),
        compiler_params=_cparams(("parallel",)),
        name="dn_gates",
    )(proj, a_log.reshape(heads, 1), dt_bias.reshape(heads, 1))


CONV_ROWS = 256


def _conv_silu(x_ref, w_ref, dst_ref, l2_scale):
    s, width = x_ref.shape
    w = w_ref[...]
    row8 = _iota2((8, width), 0)

    def block(i, _):
        r0 = pl.multiple_of(i * CONV_ROWS, CONV_ROWS)
        x = x_ref[pl.ds(r0, CONV_ROWS), :].astype(F32)
        p0 = pl.multiple_of(jnp.maximum(r0 - 16, 0), 16)
        prev = x_ref[pl.ds(p0, 16), :].astype(F32)[8:16]
        prev = prev * jnp.where(i > 0, 1.0, 0.0)
        acc = x * w[DN_CONV - 1:DN_CONV]
        for k in range(1, DN_CONV):
            xr = pltpu.roll(x, k, axis=0)
            pr = pltpu.roll(prev, k, axis=0)
            head = jnp.where(row8 < k, pr, xr[0:8])
            xr = jnp.concatenate([head, xr[8:]], axis=0)
            acc = acc + xr * w[DN_CONV - 1 - k:DN_CONV - k]
        y = acc * _sigmoid(acc)
        if l2_scale is not None:
            for c in range(width // HEAD_DIM):
                cs = slice(c * HEAD_DIM, (c + 1) * HEAD_DIM)
                yc = y[:, cs]
                ss = jnp.sum(yc * yc, axis=-1, keepdims=True)
                dst_ref[pl.ds(r0, CONV_ROWS), cs] = (yc * lax.rsqrt(ss + EPS)) * l2_scale
        else:
            dst_ref[pl.ds(r0, CONV_ROWS), :] = y
        return 0

    lax.fori_loop(0, s // CONV_ROWS, block, 0)


DN_QK_PER_STEP = 2
DN_LOCAL_PAIRS = 2


def _dn_kernel(q_ref, k_ref, v_ref, cwq_ref, cwk_ref, cwv_ref, gates_ref, og_ref, o_ref,
               qn_s, kn_s, vn_s, u0_s, wq_s, qkd_s, kdt_s, st_s):
    n_vh = 2 * DN_QK_PER_STEP
    head0 = n_vh * pl.program_id(1)
    s = q_ref.shape[0]
    n_pairs = s // DN_PAIR

    _conv_silu(q_ref, cwq_ref, qn_s, HEAD_DIM ** -0.5)
    _conv_silu(k_ref, cwk_ref, kn_s, 1.0)
    _conv_silu(v_ref, cwv_ref, vn_s, None)

    ii = _iota2((DN_PAIR, DN_PAIR), 0)
    jj = _iota2((DN_PAIR, DN_PAIR), 1)
    eye = ii == jj
    incl = ((ii < DN_CHUNK) == (jj < DN_CHUNK)) & (ii >= jj)

    def gate_row(r, vh, r0):
        return gates_ref[head0 + vh, pl.ds(r, 1), pl.ds(r0, DN_PAIR)]

    def col(x):
        return jnp.broadcast_to(x, (DN_PAIR, DN_PAIR)).T

    def mm(a, b):
        return _dot(a.astype(BF16), b.astype(BF16))

    def hs(n):
        return slice(n * HEAD_DIM, (n + 1) * HEAD_DIM)

    def local_chain(p, qh, e, q, k, qk, kk):
        r0 = pl.multiple_of(p * DN_PAIR, DN_PAIR)
        vh = 2 * qh + e
        gc_r = gate_row(1, vh, r0)
        beta_c = col(gate_row(0, vh, r0))
        gc_c = col(gc_r)
        gl_c = col(gate_row(2, vh, r0))
        decay = jnp.exp(jnp.where(incl, gc_c - gc_r, NEG_BIG))
        a = jnp.where(eye, 0.0, beta_c * kk * decay)
        x = jnp.where(eye, 1.0, -a)
        pw = mm(a, a)
        yield
        for _ in range(4):
            x, pw = x + mm(x, pw), mm(pw, pw)
            yield
        x = x + mm(x, pw)
        ep_c = jnp.exp(gc_c)
        v = vn_s[pl.ds(r0, DN_PAIR), hs(vh)]
        rhs = jnp.concatenate([v * beta_c, k * (beta_c * ep_c)], axis=1)
        yield
        sol = mm(x, rhs)
        qd = q * ep_c
        kdt = (k * jnp.exp(gl_c)).T
        qkd = qk * decay
        yield
        u0 = sol[:, :HEAD_DIM]
        w = sol[:, HEAD_DIM:]
        for cc in range(2):
            c = 2 * p + cc
            rs = slice(cc * DN_CHUNK, (cc + 1) * DN_CHUNK)
            u0_s[vh, c] = u0[rs]
            wq_s[vh, c] = jnp.concatenate([w[rs], qd[rs]], axis=0).astype(BF16)
            qkd_s[vh, c] = qkd[rs, rs].astype(BF16)
            kdt_s[vh, c] = kdt[:, rs].astype(BF16)

    def local(it, _):
        chains = []
        for pp in range(DN_LOCAL_PAIRS):
            p = DN_LOCAL_PAIRS * it + pp
            r0 = pl.multiple_of(p * DN_PAIR, DN_PAIR)
            for qh in range(DN_QK_PER_STEP):
                q = qn_s[pl.ds(r0, DN_PAIR), hs(qh)]
                k = kn_s[pl.ds(r0, DN_PAIR), hs(qh)]
                qk2 = _dot_nt(jnp.concatenate([q, k], axis=0).astype(BF16), k.astype(BF16))
                for e in range(2):
                    chains.append(local_chain(p, qh, e, q, k, qk2[:DN_PAIR], qk2[DN_PAIR:]))
        _emit_staggered(chains, 0)
        return 0

    lax.fori_loop(0, n_pairs // DN_LOCAL_PAIRS, local, 0)

    og = og_ref[...]

    def recur_head(vh, p, cc):
        r0 = pl.multiple_of(p * DN_PAIR, DN_PAIR)
        c = 2 * p + cc
        row0 = pl.multiple_of(c * DN_CHUNK, DN_CHUNK)
        r = _dot(wq_s[vh, c], st_s[vh].astype(BF16))
        yield
        u = (u0_s[vh, c] - r[:DN_CHUNK]).astype(BF16)
        o = r[DN_CHUNK:] + _dot(qkd_s[vh, c], u)
        st_s[vh] = st_s[vh] * gate_row(3 + cc, vh, r0) + _dot(kdt_s[vh, c], u)
        yield
        ms = jnp.mean(o * o, axis=-1, keepdims=True)
        o_ref[pl.ds(row0, DN_CHUNK), hs(vh)] = ((o * lax.rsqrt(ms + EPS)) * og).astype(o_ref.dtype)

    def recur(p, _):
        for cc in range(2):
            _emit_staggered([recur_head(vh, p, cc) for vh in range(n_vh)], 0)
        return 0

    st_s[...] = jnp.zeros_like(st_s)
    lax.fori_loop(0, n_pairs, recur, 0)


def _deltanet(proj, conv_w, gates, out_g, batch, q_off, k_off, v_off):
    m = proj.shape[0]
    s = m // batch
    heads = gates.shape[1]
    n_vh = 2 * DN_QK_PER_STEP
    steps = heads // n_vh
    n_chunks = s // DN_CHUNK
    qw = DN_QK_PER_STEP * HEAD_DIM
    vw = n_vh * HEAD_DIM
    qb, kb, vb = q_off // qw, k_off // qw, v_off // vw
    ck = (heads // 2) * HEAD_DIM // qw
    cv = heads * HEAD_DIM // vw
    return pl.pallas_call(
        _dn_kernel,
        grid=(batch, steps),
        in_specs=[
            pl.BlockSpec((s, qw), lambda b, h: (b, qb + h)),
            pl.BlockSpec((s, qw), lambda b, h: (b, kb + h)),
            pl.BlockSpec((s, vw), lambda b, h: (b, vb + h)),
            pl.BlockSpec((DN_CONV, qw), lambda b, h: (0, h)),
            pl.BlockSpec((DN_CONV, qw), lambda b, h: (0, ck + h)),
            pl.BlockSpec((DN_CONV, vw), lambda b, h: (0, cv + h)),
            pl.BlockSpec((None, heads, N_GATE_ROWS, s), lambda b, h: (b, 0, 0, 0)),
            pl.BlockSpec((1, HEAD_DIM), lambda b, h: (0, 0)),
        ],
        out_specs=pl.BlockSpec((s, vw), lambda b, h: (b, h)),
        out_shape=jax.ShapeDtypeStruct((m, heads * HEAD_DIM), BF16),
        scratch_shapes=[
            pltpu.VMEM((s, qw), F32),
            pltpu.VMEM((s, qw), F32),
            pltpu.VMEM((s, vw), F32),
            pltpu.VMEM((n_vh, n_chunks, DN_CHUNK, HEAD_DIM), F32),
            pltpu.VMEM((n_vh, n_chunks, 2 * DN_CHUNK, HEAD_DIM), BF16),
            pltpu.VMEM((n_vh, n_chunks, DN_CHUNK, DN_CHUNK), BF16),
            pltpu.VMEM((n_vh, n_chunks, HEAD_DIM, DN_CHUNK), BF16),
            pltpu.VMEM((n_vh, HEAD_DIM, HEAD_DIM), F32),
        ],
        compiler_params=_cparams(("parallel", "arbitrary")),
        name="gated_deltanet",
    )(proj, proj, proj, conv_w, conv_w, conv_w, gates, out_g.reshape(1, HEAD_DIM))


SB_TILE = 256


SB_DONE = -110.0
SB_SKEW = 2


def _sb_kernel(q_ref, k_ref, v_ref, qg_ref, kg_ref, o_ref, qn_s, kn_s, acc_s, carry_s):
    s = q_ref.shape[0]
    t = SB_TILE
    nq = s // t

    def normalise(x_ref, g_ref, scale, dst):
        x = x_ref[...].astype(F32)
        ms = jnp.mean(x * x, axis=-1, keepdims=True)
        dst[...] = ((x * lax.rsqrt(ms + EPS)) * g_ref[...] * scale).astype(BF16)

    normalise(q_ref, qg_ref, HEAD_DIM ** -0.5, qn_s)
    normalise(k_ref, kg_ref, 1.0, kn_s)

    ii = _iota2((t, t), 0)
    jj = _iota2((t, t), 1)
    upper = jnp.where(ii > jj, 1.0, 0.0).astype(BF16)
    causal = jj < ii

    def tile(q0, k0, diagonal):
        z = _dot_nt(qn_s[pl.ds(q0, t), :], kn_s[pl.ds(k0, t), :])
        yield
        log_beta = jnp.minimum(z, 0.0) - jnp.log(1.0 + jnp.exp(-jnp.abs(z)))
        log_rest = log_beta - z
        if diagonal:
            log_rest = jnp.where(causal, log_rest, 0.0)
        hi = log_rest.astype(BF16)
        lo = (log_rest - hi.astype(F32)).astype(BF16)
        later = _dot(hi, upper) + _dot(lo, upper)
        row_sum = jnp.sum(log_rest, axis=-1, keepdims=True)
        yield
        if diagonal:
            w = jnp.where(causal, jnp.exp(log_beta + later), 0.0)
            acc_s[pl.ds(q0, t), :] = _dot(w.astype(BF16), v_ref[pl.ds(k0, t), :])
            carry_s[pl.ds(q0, t), :] = row_sum
        else:
            carry = carry_s[pl.ds(q0, t), :]
            w = jnp.exp(log_beta + (later + carry))
            acc_s[pl.ds(q0, t), :] += _dot(w.astype(BF16), v_ref[pl.ds(k0, t), :])
            carry_s[pl.ds(q0, t), :] = carry + row_sum

    first = [tile(i * t, i * t, True) for i in range(nq)]
    second = [tile(i * t, (i - 1) * t, False) for i in range(1, nq)]
    _emit_staggered(first + second, 1, SB_SKEW)

    rows = _iota2((s, 1), 0)

    def sweep(d, _):
        live = jnp.max(jnp.where(rows >= d * t, carry_s[...], SB_DONE)) > SB_DONE

        @pl.when(live)
        def _():
            def body(i, _):
                for _ in tile(pl.multiple_of(i * t, t), pl.multiple_of((i - d) * t, t), False):
                    pass
                return 0

            lax.fori_loop(d, nq, body, 0)

        return 0

    lax.fori_loop(2, nq, sweep, 0)
    o_ref[...] = acc_s[...].astype(o_ref.dtype)


def _stickbreaking(proj, q_g, k_g, batch, heads, q_off, k_off, v_off):
    m = proj.shape[0]
    s = m // batch
    qb, kb, vb = q_off // HEAD_DIM, k_off // HEAD_DIM, v_off // HEAD_DIM
    return pl.pallas_call(
        _sb_kernel,
        grid=(batch, heads),
        in_specs=[
            pl.BlockSpec((s, HEAD_DIM), lambda b, h: (b, qb + h)),
            pl.BlockSpec((s, HEAD_DIM), lambda b, h: (b, kb + h)),
            pl.BlockSpec((s, HEAD_DIM), lambda b, h: (b, vb + h)),
            pl.BlockSpec((1, HEAD_DIM), lambda b, h: (0, 0)),
            pl.BlockSpec((1, HEAD_DIM), lambda b, h: (0, 0)),
        ],
        out_specs=pl.BlockSpec((s, HEAD_DIM), lambda b, h: (b, h)),
        out_shape=jax.ShapeDtypeStruct((m, heads * HEAD_DIM), BF16),
        scratch_shapes=[
            pltpu.VMEM((s, HEAD_DIM), BF16),
            pltpu.VMEM((s, HEAD_DIM), BF16),
            pltpu.VMEM((s, HEAD_DIM), F32),
            pltpu.VMEM((s, 1), F32),
        ],
        compiler_params=_cparams(("parallel", "parallel")),
        name="stickbreaking_attn",
    )(proj, proj, proj, q_g.reshape(1, HEAD_DIM), k_g.reshape(1, HEAD_DIM))


def kernel(x, mem, norm_g, mem_norm_g, mem_w_kv, xa_q_norm_g, xa_k_norm_g, w_out,
           dn_w_in, dn_conv_w, dn_a_log, dn_dt_bias, dn_out_norm_g,
           sb_w_in, sb_q_norm_g, sb_k_norm_g):
    batch, seq, d = x.shape
    depth = norm_g.shape[0]
    inner = w_out.shape[1]
    mix_width = inner - XA_WIDTH
    heads = mix_width // HEAD_DIM
    m = batch * seq
    tn_dn = 1280

    dn_qk_width = (heads // 2) * HEAD_DIM
    dn_ab_off = 2 * dn_qk_width + mix_width
    dn_mix_cols = dn_ab_off + 2 * heads
    dn_mix_pad = -(-(dn_ab_off + 2 * LANES) // tn_dn) * tn_dn
    sb_mix_cols = 3 * mix_width

    kv = _memkv(mem.reshape(-1, d), mem_norm_g, mem_w_kv.astype(BF16), xa_k_norm_g)
    w_out_b = w_out.astype(BF16)

    x2 = x.reshape(m, d)
    for i in range(depth):
        j = i // 2
        if i % 2 == 0:
            w_mix = jnp.pad(dn_w_in[j][:, :dn_mix_cols],
                            ((0, 0), (0, dn_mix_pad - dn_mix_cols))).astype(BF16)
            w_side = dn_w_in[j][:, dn_mix_cols:].astype(BF16)
            pmix = _inproj(x2, norm_g[i], w_mix, tm=1024, tn=tn_dn)
            side = _inproj(x2, norm_g[i], w_side, tm=1024, tn=tn_dn)
            xq_off = 0
            gates = _dn_gates(pmix, dn_a_log[j], dn_dt_bias[j], batch, dn_ab_off)
            mix = _deltanet(pmix, dn_conv_w[j], gates, dn_out_norm_g[j], batch,
                            0, dn_qk_width, 2 * dn_qk_width)
        else:
            side = _inproj(x2, norm_g[i], sb_w_in[j].astype(BF16), tm=1024, tn=1792)
            xq_off = sb_mix_cols
            mix = _stickbreaking(side, sb_q_norm_g[j], sb_k_norm_g[j], batch, heads,
                                 0, mix_width, 2 * mix_width)
        xa = _xattn(side, kv, xa_q_norm_g[i], i, batch, xq_off)
        x2 = _outproj(side, xq_off + XA_WIDTH, mix, xa, x2, w_out_b, i)
    return x2.reshape(batch, seq, d)
```

```python
import functools

import jax
import jax.numpy as jnp
from jax import lax
from jax.experimental import pallas as pl
from jax.experimental.pallas import tpu as pltpu

F32 = jnp.float32
BF16 = jnp.bfloat16
EPS = 1e-6

XA_HEADS = 4
XA_DIM = 256
XA_WIDTH = XA_HEADS * XA_DIM
HEAD_DIM = 128
DN_CONV = 4
DN_CHUNK = 64
DN_PAIR = 2 * DN_CHUNK
N_GATE_ROWS = 8

LANES = 128
VMEM_BYTES_V7X = 64 * 1024 * 1024
VMEM_LIMIT = VMEM_BYTES_V7X - 8 * 1024 * 1024

NEG_BIG = -1e30


def _cparams(sem):
    return pltpu.CompilerParams(dimension_semantics=sem, vmem_limit_bytes=VMEM_LIMIT)


def _dot(a, b):
    return jnp.dot(a, b, preferred_element_type=F32)


def _dot_nt(a, b):
    return lax.dot_general(a, b, (((1,), (1,)), ((), ())), preferred_element_type=F32)


def _split3(x):
    hi = x.astype(BF16)
    r = x - hi.astype(F32)
    mid = r.astype(BF16)
    lo = (r - mid.astype(F32)).astype(BF16)
    return hi, mid, lo


def _sigmoid(x):
    return 1.0 / (1.0 + jnp.exp(-x))


def _softplus(x):
    return jnp.maximum(x, 0.0) + jnp.log(1.0 + jnp.exp(-jnp.abs(x)))


def _iota2(shape, dim):
    return lax.broadcasted_iota(jnp.int32, shape, dim)


def _emit_staggered(items, lag, skew=1):
    live = list(enumerate(items))
    stage = {n: 0 for n, _ in live}
    t = 0
    while live:
        due = [(n, g) for n, g in live if n * lag + stage[n] * skew <= t]
        for n, g in sorted(due, key=lambda ng: -stage[ng[0]]):
            try:
                next(g)
                stage[n] += 1
            except StopIteration:
                live.remove((n, g))
        t += 1


def _rms_rows(x, g):
    ms = jnp.mean(x * x, axis=-1, keepdims=True)
    return (x * lax.rsqrt(ms + EPS)) * g


def _inproj_kernel(x_ref, g_ref, w_ref, o_ref, wb_s, *, normalise):
    @pl.when(pl.program_id(1) == 0)
    def _():
        wb_s[...] = w_ref[...].astype(BF16)

    if normalise:
        h = _rms_rows(x_ref[...], g_ref[...]).astype(BF16)
    else:
        h = x_ref[...]
    o_ref[...] = _dot(h, wb_s[...]).astype(o_ref.dtype)


def _inproj(x2, g, w, cols, tm, tn, normalise):
    m, d = x2.shape
    return pl.pallas_call(
        functools.partial(_inproj_kernel, normalise=normalise),
        grid=(cols // tn, m // tm),
        in_specs=[
            pl.BlockSpec((tm, d), lambda j, i: (i, 0)),
            pl.BlockSpec((1, d), lambda j, i: (0, 0)),
            pl.BlockSpec((d, tn), lambda j, i: (0, j)),
        ],
        out_specs=pl.BlockSpec((tm, tn), lambda j, i: (i, j)),
        out_shape=jax.ShapeDtypeStruct((m, cols), BF16),
        scratch_shapes=[pltpu.VMEM((d, tn), BF16)],
        compiler_params=_cparams(("arbitrary", "arbitrary")),
        name="rmsnorm_inproj",
    )(x2, g.reshape(1, d), w)


def _memkv_kernel(mem_ref, g_ref, w_ref, kg_ref, o_ref, *, k_tiles):
    x = mem_ref[...]
    ms = jnp.mean(x * x, axis=-1, keepdims=True)
    mn = ((x * lax.rsqrt(ms + EPS)) * g_ref[...]).astype(BF16)
    kv = _dot(mn, w_ref[0].astype(BF16))
    is_key = pl.program_id(1) < k_tiles

    @pl.when(is_key)
    def _():
        for c in range(kv.shape[1] // XA_DIM):
            kh = kv[:, c * XA_DIM:(c + 1) * XA_DIM]
            hs = jnp.mean(kh * kh, axis=-1, keepdims=True)
            o_ref[0, :, c * XA_DIM:(c + 1) * XA_DIM] = (
                (kh * lax.rsqrt(hs + EPS)) * kg_ref[0]).astype(o_ref.dtype)

    @pl.when(jnp.logical_not(is_key))
    def _():
        o_ref[0] = kv.astype(o_ref.dtype)


def _memkv(mem2, mem_g, w_kv, k_g, tn=512):
    depth, d, e = w_kv.shape
    mm = mem2.shape[0]
    return pl.pallas_call(
        functools.partial(_memkv_kernel, k_tiles=XA_WIDTH // tn),
        grid=(depth, e // tn),
        in_specs=[
            pl.BlockSpec((mm, d), lambda l, j: (0, 0)),
            pl.BlockSpec((1, d), lambda l, j: (0, 0)),
            pl.BlockSpec((1, d, tn), lambda l, j: (l, 0, j)),
            pl.BlockSpec((1, 1, XA_DIM), lambda l, j: (l, 0, 0)),
        ],
        out_specs=pl.BlockSpec((1, mm, tn), lambda l, j: (l, 0, j)),
        out_shape=jax.ShapeDtypeStruct((depth, mm, e), BF16),
        compiler_params=_cparams(("parallel", "parallel")),
        name="mem_kv",
    )(mem2, mem_g.reshape(1, d), w_kv, k_g.reshape(depth, 1, XA_DIM))


def _xattn_kernel(q_ref, qg_ref, k_ref, v_ref, o_ref):
    for h in range(XA_HEADS):
        cs = slice(h * XA_DIM, (h + 1) * XA_DIM)
        q = q_ref[:, cs].astype(F32)
        ms = jnp.mean(q * q, axis=-1, keepdims=True)
        qn = ((q * lax.rsqrt(ms + EPS)) * qg_ref[...]).astype(BF16)
        s = _dot_nt(qn, k_ref[0, :, cs]) * (XA_DIM ** -0.5)
        e = jnp.exp(s - jnp.max(s, axis=-1, keepdims=True))
        p = e / jnp.sum(e, axis=-1, keepdims=True)
        o_ref[:, cs] = _dot(p.astype(BF16), v_ref[0, :, cs]).astype(o_ref.dtype)


def _xattn(proj, kv, q_g, layer, batch, xq_off, tm=512):
    m = proj.shape[0]
    n_mem = kv.shape[1] // batch
    nt = m // batch // tm
    qb = xq_off // XA_WIDTH
    return pl.pallas_call(
        _xattn_kernel,
        grid=(batch, nt),
        in_specs=[
            pl.BlockSpec((tm, XA_WIDTH), lambda b, i: (b * nt + i, qb)),
            pl.BlockSpec((1, XA_DIM), lambda b, i: (0, 0)),
            pl.BlockSpec((1, n_mem, XA_WIDTH), lambda b, i: (layer, b, 0)),
            pl.BlockSpec((1, n_mem, XA_WIDTH), lambda b, i: (layer, b, 1)),
        ],
        out_specs=pl.BlockSpec((tm, XA_WIDTH), lambda b, i: (b * nt + i, 0)),
        out_shape=jax.ShapeDtypeStruct((m, XA_WIDTH), BF16),
        compiler_params=_cparams(("parallel", "parallel")),
        name="mem_xattn",
    )(proj, q_g.reshape(1, XA_DIM), kv, kv)


def _outproj_kernel(*refs, emit_next):
    z_refs = refs[:4]
    mix_ref, xa_ref, x_ref, w_ref = refs[4:8]
    if emit_next:
        gn_ref, o_ref, hn_ref, y_s = refs[8:]
    else:
        o_ref, y_s = refs[8:]
    zw = z_refs[0].shape[1]
    n_mix = mix_ref.shape[1] // zw
    for c, z_ref in enumerate(z_refs):
        z = z_ref[...].astype(F32)
        if c < n_mix:
            br = mix_ref[:, c * zw:(c + 1) * zw]
        else:
            br = xa_ref[:, (c - n_mix) * zw:(c - n_mix + 1) * zw]
        y_s[:, c * zw:(c + 1) * zw] = (br.astype(F32) * (z * _sigmoid(z))).astype(BF16)
    out = x_ref[...] + _dot(y_s[...], w_ref[...])
    o_ref[...] = out
    if emit_next:
        hn_ref[...] = _rms_rows(out, gn_ref[...]).astype(BF16)


def _outproj(proj, z_off, mix, xa, x2, w, layer, next_g, tm=256):
    m, d = x2.shape
    inner = w.shape[1]
    mw = mix.shape[1]
    zw = inner // 4
    zb = z_off // zw
    emit_next = next_g is not None
    row_spec = pl.BlockSpec((tm, d), lambda i: (i, 0))
    in_specs = [pl.BlockSpec((tm, zw), functools.partial(lambda c, i: (i, zb + c), c))
                for c in range(4)]
    in_specs += [
        pl.BlockSpec((tm, mw), lambda i: (i, 0)),
        pl.BlockSpec((tm, inner - mw), lambda i: (i, 0)),
        row_spec,
        pl.BlockSpec((None, inner, d), lambda i: (layer, 0, 0), pipeline_mode=pl.Buffered(1)),
    ]
    operands = [proj, proj, proj, proj, mix, xa, x2, w]
    out_specs, out_shape = row_spec, jax.ShapeDtypeStruct((m, d), F32)
    if emit_next:
        in_specs.append(pl.BlockSpec((1, d), lambda i: (0, 0)))
        operands.append(next_g.reshape(1, d))
        out_specs = (row_spec, row_spec)
        out_shape = (out_shape, jax.ShapeDtypeStruct((m, d), BF16))
    return pl.pallas_call(
        functools.partial(_outproj_kernel, emit_next=emit_next),
        grid=(m // tm,),
        in_specs=in_specs,
        out_specs=out_specs,
        out_shape=out_shape,
        scratch_shapes=[pltpu.VMEM((tm, inner), BF16)],
        compiler_params=_cparams(("parallel",)),
        name="gate_outproj",
    )(*operands)


def _dn_gates_kernel(ab_ref, alog_ref, dtb_ref, o_ref, *, heads):
    s = ab_ref.shape[0]
    t = ab_ref[:, :LANES].astype(F32).T
    a = t[0:heads]
    b = t[heads:2 * heads]
    g = -jnp.exp(alog_ref[...]) * _softplus(a + dtb_ref[...])
    beta = _sigmoid(b)

    ii = _iota2((DN_PAIR, DN_PAIR), 0)
    jj = _iota2((DN_PAIR, DN_PAIR), 1)
    lo_i = ii < DN_CHUNK
    lo_j = jj < DN_CHUNK
    same = lo_i == lo_j
    one = lambda m: jnp.where(m, 1.0, 0.0).astype(BF16)
    rhs = jnp.concatenate(
        [one(same & (ii <= jj)), one(same), one(lo_i), one(jnp.logical_not(lo_i))], axis=1)

    for p in range(s // DN_PAIR):
        ls = slice(p * DN_PAIR, (p + 1) * DN_PAIR)
        hi, mid, lo = _split3(g[:, ls])
        r = _dot(hi, rhs) + _dot(mid, rhs) + _dot(lo, rhs)
        gc = r[:, 0:DN_PAIR]
        rows = (beta[:, ls], gc, r[:, DN_PAIR:2 * DN_PAIR] - gc,
                jnp.exp(r[:, 2 * DN_PAIR:3 * DN_PAIR]), jnp.exp(r[:, 3 * DN_PAIR:4 * DN_PAIR]))
        for n, val in enumerate(rows):
            o_ref[:, n, ls] = val
        for n in range(len(rows), N_GATE_ROWS):
            o_ref[:, n, ls] = jnp.zeros_like(gc)


def _dn_gates(proj, a_log, dt_bias, batch, ab_off):
    m = proj.shape[0]
    s = m // batch
    heads = a_log.shape[0]
    blk = 2 * LANES
    return pl.pallas_call(
        functools.partial(_dn_gates_kernel, heads=heads),
        grid=(batch,),
        in_specs=[
            pl.BlockSpec((s, blk), lambda b: (b, ab_off // blk)),
            pl.BlockSpec((heads, 1), lambda b: (0, 0)),
            pl.BlockSpec((heads, 1), lambda b: (0, 0)),
        ],
        out_specs=pl.BlockSpec((None, heads, N_GATE_ROWS, s), lambda b: (b, 0, 0, 0)),
        out_shape=jax.ShapeDtypeStruct((batch, heads, N_GATE_ROWS, s), F32),
        compiler_params=_cparams(("parallel",)),
        name="dn_gates",
    )(proj, a_log.reshape(heads, 1), dt_bias.reshape(heads, 1))


CONV_ROWS = 256


def _conv_silu(x_ref, w_ref, dst_ref, l2_scale):
    s, width = x_ref.shape
    w = w_ref[...]
    row8 = _iota2((8, width), 0)

    def block(i, _):
        r0 = pl.multiple_of(i * CONV_ROWS, CONV_ROWS)
        x = x_ref[pl.ds(r0, CONV_ROWS), :].astype(F32)
        p0 = pl.multiple_of(jnp.maximum(r0 - 16, 0), 16)
        prev = x_ref[pl.ds(p0, 16), :].astype(F32)[8:16]
        prev = prev * jnp.where(i > 0, 1.0, 0.0)
        acc = x * w[DN_CONV - 1:DN_CONV]
        for k in range(1, DN_CONV):
            xr = pltpu.roll(x, k, axis=0)
            pr = pltpu.roll(prev, k, axis=0)
            head = jnp.where(row8 < k, pr, xr[0:8])
            xr = jnp.concatenate([head, xr[8:]], axis=0)
            acc = acc + xr * w[DN_CONV - 1 - k:DN_CONV - k]
        y = acc * _sigmoid(acc)
        if l2_scale is not None:
            for c in range(width // HEAD_DIM):
                cs = slice(c * HEAD_DIM, (c + 1) * HEAD_DIM)
                yc = y[:, cs]
                ss = jnp.sum(yc * yc, axis=-1, keepdims=True)
                dst_ref[pl.ds(r0, CONV_ROWS), cs] = (yc * lax.rsqrt(ss + EPS)) * l2_scale
        else:
            dst_ref[pl.ds(r0, CONV_ROWS), :] = y
        return 0

    lax.fori_loop(0, s // CONV_ROWS, block, 0)


DN_QK_PER_STEP = 2
DN_LOCAL_PAIRS = 2


def _dn_kernel(q_ref, k_ref, v_ref, cwq_ref, cwk_ref, cwv_ref, gates_ref, og_ref, o_ref,
               qn_s, kn_s, vn_s, u0_s, wq_s, qkd_s, kdt_s, st_s):
    n_vh = 2 * DN_QK_PER_STEP
    head0 = n_vh * pl.program_id(1)
    s = q_ref.shape[0]
    n_pairs = s // DN_PAIR

    _conv_silu(q_ref, cwq_ref, qn_s, HEAD_DIM ** -0.5)
    _conv_silu(k_ref, cwk_ref, kn_s, 1.0)
    _conv_silu(v_ref, cwv_ref, vn_s, None)

    ii = _iota2((DN_PAIR, DN_PAIR), 0)
    jj = _iota2((DN_PAIR, DN_PAIR), 1)
    eye = ii == jj
    incl = ((ii < DN_CHUNK) == (jj < DN_CHUNK)) & (ii >= jj)

    def gate_row(r, vh, r0):
        return gates_ref[head0 + vh, pl.ds(r, 1), pl.ds(r0, DN_PAIR)]

    def col(x):
        return jnp.broadcast_to(x, (DN_PAIR, DN_PAIR)).T

    def mm(a, b):
        return _dot(a.astype(BF16), b.astype(BF16))

    def hs(n):
        return slice(n * HEAD_DIM, (n + 1) * HEAD_DIM)

    def local_chain(p, qh, e, q, k, qk, kk):
        r0 = pl.multiple_of(p * DN_PAIR, DN_PAIR)
        vh = 2 * qh + e
        gc_r = gate_row(1, vh, r0)
        beta_c = col(gate_row(0, vh, r0))
        gc_c = col(gc_r)
        gl_c = col(gate_row(2, vh, r0))
        decay = jnp.exp(jnp.where(incl, gc_c - gc_r, NEG_BIG))
        a = jnp.where(eye, 0.0, beta_c * kk * decay)
        x = jnp.where(eye, 1.0, -a)
        pw = mm(a, a)
        yield
        for _ in range(4):
            x, pw = x + mm(x, pw), mm(pw, pw)
            yield
        x = x + mm(x, pw)
        ep_c = jnp.exp(gc_c)
        v = vn_s[pl.ds(r0, DN_PAIR), hs(vh)]
        rhs = jnp.concatenate([v * beta_c, k * (beta_c * ep_c)], axis=1)
        yield
        sol = mm(x, rhs)
        qd = q * ep_c
        kdt = (k * jnp.exp(gl_c)).T
        qkd = qk * decay
        yield
        u0 = sol[:, :HEAD_DIM]
        w = sol[:, HEAD_DIM:]
        for cc in range(2):
            c = 2 * p + cc
            rs = slice(cc * DN_CHUNK, (cc + 1) * DN_CHUNK)
            u0_s[vh, c] = u0[rs]
            wq_s[vh, c] = jnp.concatenate([w[rs], qd[rs]], axis=0).astype(BF16)
            qkd_s[vh, c] = qkd[rs, rs].astype(BF16)
            kdt_s[vh, c] = kdt[:, rs].astype(BF16)

    def local(it, _):
        chains = []
        for pp in range(DN_LOCAL_PAIRS):
            p = DN_LOCAL_PAIRS * it + pp
            r0 = pl.multiple_of(p * DN_PAIR, DN_PAIR)
            for qh in range(DN_QK_PER_STEP):
                q = qn_s[pl.ds(r0, DN_PAIR), hs(qh)]
                k = kn_s[pl.ds(r0, DN_PAIR), hs(qh)]
                qk2 = _dot_nt(jnp.concatenate([q, k], axis=0).astype(BF16), k.astype(BF16))
                for e in range(2):
                    chains.append(local_chain(p, qh, e, q, k, qk2[:DN_PAIR], qk2[DN_PAIR:]))
        _emit_staggered(chains, 0)
        return 0

    lax.fori_loop(0, n_pairs // DN_LOCAL_PAIRS, local, 0)

    og = og_ref[...]

    def recur_head(vh, p, cc):
        r0 = pl.multiple_of(p * DN_PAIR, DN_PAIR)
        c = 2 * p + cc
        row0 = pl.multiple_of(c * DN_CHUNK, DN_CHUNK)
        r = _dot(wq_s[vh, c], st_s[vh].astype(BF16))
        yield
        u = (u0_s[vh, c] - r[:DN_CHUNK]).astype(BF16)
        o = r[DN_CHUNK:] + _dot(qkd_s[vh, c], u)
        st_s[vh] = st_s[vh] * gate_row(3 + cc, vh, r0) + _dot(kdt_s[vh, c], u)
        yield
        ms = jnp.mean(o * o, axis=-1, keepdims=True)
        o_ref[pl.ds(row0, DN_CHUNK), hs(vh)] = ((o * lax.rsqrt(ms + EPS)) * og).astype(o_ref.dtype)

    def recur(p, _):
        for cc in range(2):
            _emit_staggered([recur_head(vh, p, cc) for vh in range(n_vh)], 0)
        return 0

    st_s[...] = jnp.zeros_like(st_s)
    lax.fori_loop(0, n_pairs, recur, 0)


def _deltanet(proj, conv_w, gates, out_g, batch, q_off, k_off, v_off):
    m = proj.shape[0]
    s = m // batch
    heads = gates.shape[1]
    n_vh = 2 * DN_QK_PER_STEP
    steps = heads // n_vh
    n_chunks = s // DN_CHUNK
    qw = DN_QK_PER_STEP * HEAD_DIM
    vw = n_vh * HEAD_DIM
    qb, kb, vb = q_off // qw, k_off // qw, v_off // vw
    ck = (heads // 2) * HEAD_DIM // qw
    cv = heads * HEAD_DIM // vw
    return pl.pallas_call(
        _dn_kernel,
        grid=(batch, steps),
        in_specs=[
            pl.BlockSpec((s, qw), lambda b, h: (b, qb + h)),
            pl.BlockSpec((s, qw), lambda b, h: (b, kb + h)),
            pl.BlockSpec((s, vw), lambda b, h: (b, vb + h)),
            pl.BlockSpec((DN_CONV, qw), lambda b, h: (0, h)),
            pl.BlockSpec((DN_CONV, qw), lambda b, h: (0, ck + h)),
            pl.BlockSpec((DN_CONV, vw), lambda b, h: (0, cv + h)),
            pl.BlockSpec((None, heads, N_GATE_ROWS, s), lambda b, h: (b, 0, 0, 0)),
            pl.BlockSpec((1, HEAD_DIM), lambda b, h: (0, 0)),
        ],
        out_specs=pl.BlockSpec((s, vw), lambda b, h: (b, h)),
        out_shape=jax.ShapeDtypeStruct((m, heads * HEAD_DIM), BF16),
        scratch_shapes=[
            pltpu.VMEM((s, qw), F32),
            pltpu.VMEM((s, qw), F32),
            pltpu.VMEM((s, vw), F32),
            pltpu.VMEM((n_vh, n_chunks, DN_CHUNK, HEAD_DIM), F32),
            pltpu.VMEM((n_vh, n_chunks, 2 * DN_CHUNK, HEAD_DIM), BF16),
            pltpu.VMEM((n_vh, n_chunks, DN_CHUNK, DN_CHUNK), BF16),
            pltpu.VMEM((n_vh, n_chunks, HEAD_DIM, DN_CHUNK), BF16),
            pltpu.VMEM((n_vh, HEAD_DIM, HEAD_DIM), F32),
        ],
        compiler_params=_cparams(("parallel", "arbitrary")),
        name="gated_deltanet",
    )(proj, proj, proj, conv_w, conv_w, conv_w, gates, out_g.reshape(1, HEAD_DIM))


SB_TILE = 256


SB_DONE = -110.0
SB_SKEW = 1


def _sb_kernel(q_ref, k_ref, v_ref, qg_ref, kg_ref, o_ref, qn_s, kn_s, acc_s, carry_s):
    s = q_ref.shape[0]
    t = SB_TILE
    nq = s // t

    def normalise(x_ref, g_ref, scale, dst):
        x = x_ref[...].astype(F32)
        ms = jnp.mean(x * x, axis=-1, keepdims=True)
        dst[...] = ((x * lax.rsqrt(ms + EPS)) * g_ref[...] * scale).astype(BF16)

    normalise(q_ref, qg_ref, HEAD_DIM ** -0.5, qn_s)
    normalise(k_ref, kg_ref, 1.0, kn_s)

    ii = _iota2((t, t), 0)
    jj = _iota2((t, t), 1)
    upper = jnp.where(ii > jj, 1.0, 0.0).astype(BF16)
    causal = jj < ii

    def tile(q0, k0, diagonal):
        z = _dot_nt(qn_s[pl.ds(q0, t), :], kn_s[pl.ds(k0, t), :])
        yield
        log_beta = jnp.minimum(z, 0.0) - jnp.log(1.0 + jnp.exp(-jnp.abs(z)))
        log_rest = log_beta - z
        if diagonal:
            log_rest = jnp.where(causal, log_rest, 0.0)
        hi = log_rest.astype(BF16)
        lo = (log_rest - hi.astype(F32)).astype(BF16)
        later = _dot(hi, upper) + _dot(lo, upper)
        row_sum = jnp.sum(log_rest, axis=-1, keepdims=True)
        yield
        if diagonal:
            w = jnp.where(causal, jnp.exp(log_beta + later), 0.0)
            acc_s[pl.ds(q0, t), :] = _dot(w.astype(BF16), v_ref[pl.ds(k0, t), :])
            carry_s[pl.ds(q0, t), :] = row_sum
        else:
            carry = carry_s[pl.ds(q0, t), :]
            w = jnp.exp(log_beta + (later + carry))
            acc_s[pl.ds(q0, t), :] += _dot(w.astype(BF16), v_ref[pl.ds(k0, t), :])
            carry_s[pl.ds(q0, t), :] = carry + row_sum

    first = [tile(i * t, i * t, True) for i in range(nq)]
    second = [tile(i * t, (i - 1) * t, False) for i in range(1, nq)]
    _emit_staggered(first + second, 1, SB_SKEW)

    rows = _iota2((s, 1), 0)

    def sweep(d, _):
        live = jnp.max(jnp.where(rows >= d * t, carry_s[...], SB_DONE)) > SB_DONE

        @pl.when(live)
        def _():
            def body(i, _):
                for _ in tile(pl.multiple_of(i * t, t), pl.multiple_of((i - d) * t, t), False):
                    pass
                return 0

            lax.fori_loop(d, nq, body, 0)

        return 0

    lax.fori_loop(2, nq, sweep, 0)
    o_ref[...] = acc_s[...].astype(o_ref.dtype)


def _stickbreaking(proj, q_g, k_g, batch, heads, q_off, k_off, v_off):
    m = proj.shape[0]
    s = m // batch
    qb, kb, vb = q_off // HEAD_DIM, k_off // HEAD_DIM, v_off // HEAD_DIM
    return pl.pallas_call(
        _sb_kernel,
        grid=(batch, heads),
        in_specs=[
            pl.BlockSpec((s, HEAD_DIM), lambda b, h: (b, qb + h)),
            pl.BlockSpec((s, HEAD_DIM), lambda b, h: (b, kb + h)),
            pl.BlockSpec((s, HEAD_DIM), lambda b, h: (b, vb + h)),
            pl.BlockSpec((1, HEAD_DIM), lambda b, h: (0, 0)),
            pl.BlockSpec((1, HEAD_DIM), lambda b, h: (0, 0)),
        ],
        out_specs=pl.BlockSpec((s, HEAD_DIM), lambda b, h: (b, h)),
        out_shape=jax.ShapeDtypeStruct((m, heads * HEAD_DIM), BF16),
        scratch_shapes=[
            pltpu.VMEM((s, HEAD_DIM), BF16),
            pltpu.VMEM((s, HEAD_DIM), BF16),
            pltpu.VMEM((s, HEAD_DIM), F32),
            pltpu.VMEM((s, 1), F32),
        ],
        compiler_params=_cparams(("parallel", "parallel")),
        name="stickbreaking_attn",
    )(proj, proj, proj, q_g.reshape(1, HEAD_DIM), k_g.reshape(1, HEAD_DIM))


def kernel(x, mem, norm_g, mem_norm_g, mem_w_kv, xa_q_norm_g, xa_k_norm_g, w_out,
           dn_w_in, dn_conv_w, dn_a_log, dn_dt_bias, dn_out_norm_g,
           sb_w_in, sb_q_norm_g, sb_k_norm_g):
    batch, seq, d = x.shape
    depth = norm_g.shape[0]
    inner = w_out.shape[1]
    mix_width = inner - XA_WIDTH
    heads = mix_width // HEAD_DIM
    m = batch * seq
    tn_dn = 1280

    dn_qk_width = (heads // 2) * HEAD_DIM
    dn_ab_off = 2 * dn_qk_width + mix_width
    dn_mix_cols = dn_ab_off + 2 * heads
    dn_mix_pad = -(-(dn_ab_off + 2 * LANES) // tn_dn) * tn_dn
    sb_mix_cols = 3 * mix_width

    kv = _memkv(mem.reshape(-1, d), mem_norm_g, mem_w_kv, xa_k_norm_g)
    w_out_b = w_out.astype(BF16)

    x2 = x.reshape(m, d)
    h = None
    for i in range(depth):
        j = i // 2
        src, g, normalise = (x2, norm_g[i], True) if h is None else (h, norm_g[i], False)
        if i % 2 == 0:
            w_side = dn_w_in[j][:, dn_mix_cols:].astype(BF16)
            pmix = _inproj(src, g, dn_w_in[j], dn_mix_pad, 1024, tn_dn, normalise)
            side = _inproj(src, g, w_side, w_side.shape[1], 1024, tn_dn, normalise)
            xq_off = 0
            gates = _dn_gates(pmix, dn_a_log[j], dn_dt_bias[j], batch, dn_ab_off)
            mix = _deltanet(pmix, dn_conv_w[j], gates, dn_out_norm_g[j], batch,
                            0, dn_qk_width, 2 * dn_qk_width)
        else:
            side = _inproj(src, g, sb_w_in[j], sb_w_in.shape[2], 1024, 1024, normalise)
            xq_off = sb_mix_cols
            mix = _stickbreaking(side, sb_q_norm_g[j], sb_k_norm_g[j], batch, heads,
                                 0, mix_width, 2 * mix_width)
        xa = _xattn(side, kv, xa_q_norm_g[i], i, batch, xq_off)
        next_g = norm_g[i + 1] if i + 1 < depth else None
        res = _outproj(side, xq_off + XA_WIDTH, mix, xa, x2, w_out_b, i, next_g)
        x2, h = res if next_g is not None else (res, None)
    return x2.reshape(batch, seq, d)
```

```python
import functools

import jax
import jax.numpy as jnp
from jax import lax
from jax.experimental import pallas as pl
from jax.experimental.pallas import tpu as pltpu

F32 = jnp.float32
BF16 = jnp.bfloat16
EPS = 1e-6

XA_HEADS = 4
XA_DIM = 256
XA_WIDTH = XA_HEADS * XA_DIM
HEAD_DIM = 128
DN_CONV = 4
DN_CHUNK = 64
DN_PAIR = 2 * DN_CHUNK
N_GATE_ROWS = 8

LANES = 128
VMEM_BYTES_V7X = 64 * 1024 * 1024
VMEM_LIMIT = VMEM_BYTES_V7X - 8 * 1024 * 1024

NEG_BIG = -1e30


def _cparams(sem):
    return pltpu.CompilerParams(dimension_semantics=sem, vmem_limit_bytes=VMEM_LIMIT)


def _dot(a, b):
    return jnp.dot(a, b, preferred_element_type=F32)


def _dot_nt(a, b):
    return lax.dot_general(a, b, (((1,), (1,)), ((), ())), preferred_element_type=F32)


def _split3(x):
    hi = x.astype(BF16)
    r = x - hi.astype(F32)
    mid = r.astype(BF16)
    lo = (r - mid.astype(F32)).astype(BF16)
    return hi, mid, lo


def _sigmoid(x):
    return 1.0 / (1.0 + jnp.exp(-x))


def _softplus(x):
    return jnp.maximum(x, 0.0) + jnp.log(1.0 + jnp.exp(-jnp.abs(x)))


def _iota2(shape, dim):
    return lax.broadcasted_iota(jnp.int32, shape, dim)


def _emit_staggered(items, lag, skew=1):
    live = list(enumerate(items))
    stage = {n: 0 for n, _ in live}
    t = 0
    while live:
        due = [(n, g) for n, g in live if n * lag + stage[n] * skew <= t]
        for n, g in sorted(due, key=lambda ng: -stage[ng[0]]):
            try:
                next(g)
                stage[n] += 1
            except StopIteration:
                live.remove((n, g))
        t += 1


def _rms_rows(x, g):
    ms = jnp.mean(x * x, axis=-1, keepdims=True)
    return (x * lax.rsqrt(ms + EPS)) * g


def _inproj_kernel(x_ref, g_ref, w_ref, o_ref, wb_s, *, normalise):
    @pl.when(pl.program_id(1) == 0)
    def _():
        wb_s[...] = w_ref[...].astype(BF16)

    if normalise:
        h = _rms_rows(x_ref[...], g_ref[...]).astype(BF16)
    else:
        h = x_ref[...]
    o_ref[...] = _dot(h, wb_s[...]).astype(o_ref.dtype)


def _inproj(x2, g, w, cols, tm, tn, normalise):
    m, d = x2.shape
    return pl.pallas_call(
        functools.partial(_inproj_kernel, normalise=normalise),
        grid=(cols // tn, m // tm),
        in_specs=[
            pl.BlockSpec((tm, d), lambda j, i: (i, 0)),
            pl.BlockSpec((1, d), lambda j, i: (0, 0)),
            pl.BlockSpec((d, tn), lambda j, i: (0, j)),
        ],
        out_specs=pl.BlockSpec((tm, tn), lambda j, i: (i, j)),
        out_shape=jax.ShapeDtypeStruct((m, cols), BF16),
        scratch_shapes=[pltpu.VMEM((d, tn), BF16)],
        compiler_params=_cparams(("arbitrary", "arbitrary")),
        name="rmsnorm_inproj",
    )(x2, g.reshape(1, d), w)


def _memkv_kernel(mem_ref, g_ref, w_ref, kg_ref, o_ref, *, k_tiles):
    x = mem_ref[...]
    ms = jnp.mean(x * x, axis=-1, keepdims=True)
    mn = ((x * lax.rsqrt(ms + EPS)) * g_ref[...]).astype(BF16)
    kv = _dot(mn, w_ref[0].astype(BF16))
    is_key = pl.program_id(1) < k_tiles

    @pl.when(is_key)
    def _():
        for c in range(kv.shape[1] // XA_DIM):
            kh = kv[:, c * XA_DIM:(c + 1) * XA_DIM]
            hs = jnp.mean(kh * kh, axis=-1, keepdims=True)
            o_ref[0, :, c * XA_DIM:(c + 1) * XA_DIM] = (
                (kh * lax.rsqrt(hs + EPS)) * kg_ref[0]).astype(o_ref.dtype)

    @pl.when(jnp.logical_not(is_key))
    def _():
        o_ref[0] = kv.astype(o_ref.dtype)


def _memkv(mem2, mem_g, w_kv, k_g, tn=512):
    depth, d, e = w_kv.shape
    mm = mem2.shape[0]
    return pl.pallas_call(
        functools.partial(_memkv_kernel, k_tiles=XA_WIDTH // tn),
        grid=(depth, e // tn),
        in_specs=[
            pl.BlockSpec((mm, d), lambda l, j: (0, 0)),
            pl.BlockSpec((1, d), lambda l, j: (0, 0)),
            pl.BlockSpec((1, d, tn), lambda l, j: (l, 0, j)),
            pl.BlockSpec((1, 1, XA_DIM), lambda l, j: (l, 0, 0)),
        ],
        out_specs=pl.BlockSpec((1, mm, tn), lambda l, j: (l, 0, j)),
        out_shape=jax.ShapeDtypeStruct((depth, mm, e), BF16),
        compiler_params=_cparams(("parallel", "parallel")),
        name="mem_kv",
    )(mem2, mem_g.reshape(1, d), w_kv, k_g.reshape(depth, 1, XA_DIM))


def _xattn_kernel(q_ref, qg_ref, k_ref, v_ref, o_ref):
    for h in range(XA_HEADS):
        cs = slice(h * XA_DIM, (h + 1) * XA_DIM)
        q = q_ref[:, cs].astype(F32)
        ms = jnp.mean(q * q, axis=-1, keepdims=True)
        qn = ((q * lax.rsqrt(ms + EPS)) * qg_ref[...]).astype(BF16)
        s = _dot_nt(qn, k_ref[0, :, cs]) * (XA_DIM ** -0.5)
        e = jnp.exp(s - jnp.max(s, axis=-1, keepdims=True))
        p = e / jnp.sum(e, axis=-1, keepdims=True)
        o_ref[:, cs] = _dot(p.astype(BF16), v_ref[0, :, cs]).astype(o_ref.dtype)


def _xattn(proj, kv, q_g, layer, batch, xq_off, tm=512):
    m = proj.shape[0]
    n_mem = kv.shape[1] // batch
    nt = m // batch // tm
    qb = xq_off // XA_WIDTH
    return pl.pallas_call(
        _xattn_kernel,
        grid=(batch, nt),
        in_specs=[
            pl.BlockSpec((tm, XA_WIDTH), lambda b, i: (b * nt + i, qb)),
            pl.BlockSpec((1, XA_DIM), lambda b, i: (0, 0)),
            pl.BlockSpec((1, n_mem, XA_WIDTH), lambda b, i: (layer, b, 0)),
            pl.BlockSpec((1, n_mem, XA_WIDTH), lambda b, i: (layer, b, 1)),
        ],
        out_specs=pl.BlockSpec((tm, XA_WIDTH), lambda b, i: (b * nt + i, 0)),
        out_shape=jax.ShapeDtypeStruct((m, XA_WIDTH), BF16),
        compiler_params=_cparams(("parallel", "parallel")),
        name="mem_xattn",
    )(proj, q_g.reshape(1, XA_DIM), kv, kv)


def _outproj_kernel(*refs, emit_next):
    z_refs = refs[:4]
    mix_ref, xa_ref, x_ref, w_ref = refs[4:8]
    if emit_next:
        gn_ref, o_ref, hn_ref, y_s = refs[8:]
    else:
        o_ref, y_s = refs[8:]
    zw = z_refs[0].shape[1]
    n_mix = mix_ref.shape[1] // zw
    for c, z_ref in enumerate(z_refs):
        z = z_ref[...].astype(F32)
        if c < n_mix:
            br = mix_ref[:, c * zw:(c + 1) * zw]
        else:
            br = xa_ref[:, (c - n_mix) * zw:(c - n_mix + 1) * zw]
        y_s[:, c * zw:(c + 1) * zw] = (br.astype(F32) * (z * _sigmoid(z))).astype(BF16)
    out = x_ref[...] + _dot(y_s[...], w_ref[...])
    o_ref[...] = out
    if emit_next:
        hn_ref[...] = _rms_rows(out, gn_ref[...]).astype(BF16)


def _outproj(proj, z_off, mix, xa, x2, w, layer, next_g, tm=256):
    m, d = x2.shape
    inner = w.shape[1]
    mw = mix.shape[1]
    zw = inner // 4
    zb = z_off // zw
    emit_next = next_g is not None
    row_spec = pl.BlockSpec((tm, d), lambda i: (i, 0))
    in_specs = [pl.BlockSpec((tm, zw), functools.partial(lambda c, i: (i, zb + c), c))
                for c in range(4)]
    in_specs += [
        pl.BlockSpec((tm, mw), lambda i: (i, 0)),
        pl.BlockSpec((tm, inner - mw), lambda i: (i, 0)),
        row_spec,
        pl.BlockSpec((None, inner, d), lambda i: (layer, 0, 0), pipeline_mode=pl.Buffered(1)),
    ]
    operands = [proj, proj, proj, proj, mix, xa, x2, w]
    out_specs, out_shape = row_spec, jax.ShapeDtypeStruct((m, d), F32)
    if emit_next:
        in_specs.append(pl.BlockSpec((1, d), lambda i: (0, 0)))
        operands.append(next_g.reshape(1, d))
        out_specs = (row_spec, row_spec)
        out_shape = (out_shape, jax.ShapeDtypeStruct((m, d), BF16))
    return pl.pallas_call(
        functools.partial(_outproj_kernel, emit_next=emit_next),
        grid=(m // tm,),
        in_specs=in_specs,
        out_specs=out_specs,
        out_shape=out_shape,
        scratch_shapes=[pltpu.VMEM((tm, inner), BF16)],
        compiler_params=_cparams(("parallel",)),
        name="gate_outproj",
    )(*operands)


def _dn_gates_kernel(ab_ref, alog_ref, dtb_ref, o_ref, *, heads):
    s = ab_ref.shape[0]
    t = ab_ref[:, :LANES].astype(F32).T
    a = t[0:heads]
    b = t[heads:2 * heads]
    g = -jnp.exp(alog_ref[...]) * _softplus(a + dtb_ref[...])
    beta = _sigmoid(b)

    ii = _iota2((DN_PAIR, DN_PAIR), 0)
    jj = _iota2((DN_PAIR, DN_PAIR), 1)
    lo_i = ii < DN_CHUNK
    lo_j = jj < DN_CHUNK
    same = lo_i == lo_j
    one = lambda m: jnp.where(m, 1.0, 0.0).astype(BF16)
    rhs = jnp.concatenate(
        [one(same & (ii <= jj)), one(same), one(lo_i), one(jnp.logical_not(lo_i))], axis=1)

    for p in range(s // DN_PAIR):
        ls = slice(p * DN_PAIR, (p + 1) * DN_PAIR)
        hi, mid, lo = _split3(g[:, ls])
        r = _dot(hi, rhs) + _dot(mid, rhs) + _dot(lo, rhs)
        gc = r[:, 0:DN_PAIR]
        rows = (beta[:, ls], gc, r[:, DN_PAIR:2 * DN_PAIR] - gc,
                jnp.exp(r[:, 2 * DN_PAIR:3 * DN_PAIR]), jnp.exp(r[:, 3 * DN_PAIR:4 * DN_PAIR]))
        for n, val in enumerate(rows):
            o_ref[:, n, ls] = val
        for n in range(len(rows), N_GATE_ROWS):
            o_ref[:, n, ls] = jnp.zeros_like(gc)


def _dn_gates(proj, a_log, dt_bias, batch, ab_off):
    m = proj.shape[0]
    s = m // batch
    heads = a_log.shape[0]
    blk = 2 * LANES
    return pl.pallas_call(
        functools.partial(_dn_gates_kernel, heads=heads),
        grid=(batch,),
        in_specs=[
            pl.BlockSpec((s, blk), lambda b: (b, ab_off // blk)),
            pl.BlockSpec((heads, 1), lambda b: (0, 0)),
            pl.BlockSpec((heads, 1), lambda b: (0, 0)),
        ],
        out_specs=pl.BlockSpec((None, heads, N_GATE_ROWS, s), lambda b: (b, 0, 0, 0)),
        out_shape=jax.ShapeDtypeStruct((batch, heads, N_GATE_ROWS, s), F32),
        compiler_params=_cparams(("parallel",)),
        name="dn_gates",
    )(proj, a_log.reshape(heads, 1), dt_bias.reshape(heads, 1))


CONV_ROWS = 256


def _conv_silu(x_ref, w_ref, dst_ref, l2_scale):
    s, width = x_ref.shape
    w = w_ref[...]
    row8 = _iota2((8, width), 0)

    def block(i, _):
        r0 = pl.multiple_of(i * CONV_ROWS, CONV_ROWS)
        x = x_ref[pl.ds(r0, CONV_ROWS), :].astype(F32)
        p0 = pl.multiple_of(jnp.maximum(r0 - 16, 0), 16)
        prev = x_ref[pl.ds(p0, 16), :].astype(F32)[8:16]
        prev = prev * jnp.where(i > 0, 1.0, 0.0)
        acc = x * w[DN_CONV - 1:DN_CONV]
        for k in range(1, DN_CONV):
            xr = pltpu.roll(x, k, axis=0)
            pr = pltpu.roll(prev, k, axis=0)
            head = jnp.where(row8 < k, pr, xr[0:8])
            xr = jnp.concatenate([head, xr[8:]], axis=0)
            acc = acc + xr * w[DN_CONV - 1 - k:DN_CONV - k]
        y = acc * _sigmoid(acc)
        if l2_scale is not None:
            for c in range(width // HEAD_DIM):
                cs = slice(c * HEAD_DIM, (c + 1) * HEAD_DIM)
                yc = y[:, cs]
                ss = jnp.sum(yc * yc, axis=-1, keepdims=True)
                dst_ref[pl.ds(r0, CONV_ROWS), cs] = (yc * lax.rsqrt(ss + EPS)) * l2_scale
        else:
            dst_ref[pl.ds(r0, CONV_ROWS), :] = y
        return 0

    lax.fori_loop(0, s // CONV_ROWS, block, 0)


DN_QK_PER_STEP = 2
DN_LOCAL_PAIRS = 2


def _dn_kernel(q_ref, k_ref, v_ref, cwq_ref, cwk_ref, cwv_ref, gates_ref, og_ref, o_ref,
               qn_s, kn_s, vn_s, u0_s, wq_s, qkd_s, kdt_s, st_s):
    n_vh = 2 * DN_QK_PER_STEP
    head0 = n_vh * pl.program_id(1)
    s = q_ref.shape[0]
    n_pairs = s // DN_PAIR

    _conv_silu(q_ref, cwq_ref, qn_s, HEAD_DIM ** -0.5)
    _conv_silu(k_ref, cwk_ref, kn_s, 1.0)
    _conv_silu(v_ref, cwv_ref, vn_s, None)

    ii = _iota2((DN_PAIR, DN_PAIR), 0)
    jj = _iota2((DN_PAIR, DN_PAIR), 1)
    eye = ii == jj
    incl = ((ii < DN_CHUNK) == (jj < DN_CHUNK)) & (ii >= jj)

    def gate_row(r, vh, r0):
        return gates_ref[head0 + vh, pl.ds(r, 1), pl.ds(r0, DN_PAIR)]

    def col(x):
        return jnp.broadcast_to(x, (DN_PAIR, DN_PAIR)).T

    def mm(a, b):
        return _dot(a.astype(BF16), b.astype(BF16))

    def hs(n):
        return slice(n * HEAD_DIM, (n + 1) * HEAD_DIM)

    def local_chain(p, qh, e, q, k, qk, kk):
        r0 = pl.multiple_of(p * DN_PAIR, DN_PAIR)
        vh = 2 * qh + e
        gc_r = gate_row(1, vh, r0)
        beta_c = col(gate_row(0, vh, r0))
        gc_c = col(gc_r)
        gl_c = col(gate_row(2, vh, r0))
        decay = jnp.exp(jnp.where(incl, gc_c - gc_r, NEG_BIG))
        a = jnp.where(eye, 0.0, beta_c * kk * decay)
        x = jnp.where(eye, 1.0, -a)
        pw = mm(a, a)
        yield
        for _ in range(4):
            x, pw = x + mm(x, pw), mm(pw, pw)
            yield
        x = x + mm(x, pw)
        ep_c = jnp.exp(gc_c)
        v = vn_s[pl.ds(r0, DN_PAIR), hs(vh)]
        rhs = jnp.concatenate([v * beta_c, k * (beta_c * ep_c)], axis=1)
        yield
        sol = mm(x, rhs)
        qd = q * ep_c
        kdt = (k * jnp.exp(gl_c)).T
        qkd = qk * decay
        yield
        u0 = sol[:, :HEAD_DIM]
        w = sol[:, HEAD_DIM:]
        for cc in range(2):
            c = 2 * p + cc
            rs = slice(cc * DN_CHUNK, (cc + 1) * DN_CHUNK)
            u0_s[vh, c] = u0[rs]
            wq_s[vh, c] = jnp.concatenate([w[rs], qd[rs]], axis=0).astype(BF16)
            qkd_s[vh, c] = qkd[rs, rs].astype(BF16)
            kdt_s[vh, c] = kdt[:, rs].astype(BF16)

    def local(it, _):
        chains = []
        for pp in range(DN_LOCAL_PAIRS):
            p = DN_LOCAL_PAIRS * it + pp
            r0 = pl.multiple_of(p * DN_PAIR, DN_PAIR)
            for qh in range(DN_QK_PER_STEP):
                q = qn_s[pl.ds(r0, DN_PAIR), hs(qh)]
                k = kn_s[pl.ds(r0, DN_PAIR), hs(qh)]
                qk2 = _dot_nt(jnp.concatenate([q, k], axis=0).astype(BF16), k.astype(BF16))
                for e in range(2):
                    chains.append(local_chain(p, qh, e, q, k, qk2[:DN_PAIR], qk2[DN_PAIR:]))
        _emit_staggered(chains, 0)
        return 0

    lax.fori_loop(0, n_pairs // DN_LOCAL_PAIRS, local, 0)

    og = og_ref[...]

    def recur_head(vh, p, cc):
        r0 = pl.multiple_of(p * DN_PAIR, DN_PAIR)
        c = 2 * p + cc
        row0 = pl.multiple_of(c * DN_CHUNK, DN_CHUNK)
        r = _dot(wq_s[vh, c], st_s[vh].astype(BF16))
        yield
        u = (u0_s[vh, c] - r[:DN_CHUNK]).astype(BF16)
        o = r[DN_CHUNK:] + _dot(qkd_s[vh, c], u)
        st_s[vh] = st_s[vh] * gate_row(3 + cc, vh, r0) + _dot(kdt_s[vh, c], u)
        yield
        ms = jnp.mean(o * o, axis=-1, keepdims=True)
        o_ref[pl.ds(row0, DN_CHUNK), hs(vh)] = ((o * lax.rsqrt(ms + EPS)) * og).astype(o_ref.dtype)

    def recur(p, _):
        for cc in range(2):
            _emit_staggered([recur_head(vh, p, cc) for vh in range(n_vh)], 0)
        return 0

    st_s[...] = jnp.zeros_like(st_s)
    lax.fori_loop(0, n_pairs, recur, 0)


def _deltanet(proj, conv_w, gates, out_g, batch, q_off, k_off, v_off):
    m = proj.shape[0]
    s = m // batch
    heads = gates.shape[1]
    n_vh = 2 * DN_QK_PER_STEP
    steps = heads // n_vh
    n_chunks = s // DN_CHUNK
    qw = DN_QK_PER_STEP * HEAD_DIM
    vw = n_vh * HEAD_DIM
    qb, kb, vb = q_off // qw, k_off // qw, v_off // vw
    ck = (heads // 2) * HEAD_DIM // qw
    cv = heads * HEAD_DIM // vw
    return pl.pallas_call(
        _dn_kernel,
        grid=(batch, steps),
        in_specs=[
            pl.BlockSpec((s, qw), lambda b, h: (b, qb + h)),
            pl.BlockSpec((s, qw), lambda b, h: (b, kb + h)),
            pl.BlockSpec((s, vw), lambda b, h: (b, vb + h)),
            pl.BlockSpec((DN_CONV, qw), lambda b, h: (0, h)),
            pl.BlockSpec((DN_CONV, qw), lambda b, h: (0, ck + h)),
            pl.BlockSpec((DN_CONV, vw), lambda b, h: (0, cv + h)),
            pl.BlockSpec((None, heads, N_GATE_ROWS, s), lambda b, h: (b, 0, 0, 0)),
            pl.BlockSpec((1, HEAD_DIM), lambda b, h: (0, 0)),
        ],
        out_specs=pl.BlockSpec((s, vw), lambda b, h: (b, h)),
        out_shape=jax.ShapeDtypeStruct((m, heads * HEAD_DIM), BF16),
        scratch_shapes=[
            pltpu.VMEM((s, qw), F32),
            pltpu.VMEM((s, qw), F32),
            pltpu.VMEM((s, vw), F32),
            pltpu.VMEM((n_vh, n_chunks, DN_CHUNK, HEAD_DIM), F32),
            pltpu.VMEM((n_vh, n_chunks, 2 * DN_CHUNK, HEAD_DIM), BF16),
            pltpu.VMEM((n_vh, n_chunks, DN_CHUNK, DN_CHUNK), BF16),
            pltpu.VMEM((n_vh, n_chunks, HEAD_DIM, DN_CHUNK), BF16),
            pltpu.VMEM((n_vh, HEAD_DIM, HEAD_DIM), F32),
        ],
        compiler_params=_cparams(("parallel", "arbitrary")),
        name="gated_deltanet",
    )(proj, proj, proj, conv_w, conv_w, conv_w, gates, out_g.reshape(1, HEAD_DIM))


SB_TILE = 256


SB_DONE = -110.0
SB_SKEW = 3


def _sb_kernel(q_ref, k_ref, v_ref, qg_ref, kg_ref, o_ref, qn_s, kn_s, acc_s, carry_s):
    s = q_ref.shape[0]
    t = SB_TILE
    nq = s // t

    def normalise(x_ref, g_ref, scale, dst):
        x = x_ref[...].astype(F32)
        ms = jnp.mean(x * x, axis=-1, keepdims=True)
        dst[...] = ((x * lax.rsqrt(ms + EPS)) * g_ref[...] * scale).astype(BF16)

    normalise(q_ref, qg_ref, HEAD_DIM ** -0.5, qn_s)
    normalise(k_ref, kg_ref, 1.0, kn_s)

    ii = _iota2((t, t), 0)
    jj = _iota2((t, t), 1)
    upper = jnp.where(ii > jj, 1.0, 0.0).astype(BF16)
    causal = jj < ii

    def tile(q0, k0, diagonal):
        z = _dot_nt(qn_s[pl.ds(q0, t), :], kn_s[pl.ds(k0, t), :])
        yield
        log_beta = jnp.minimum(z, 0.0) - jnp.log(1.0 + jnp.exp(-jnp.abs(z)))
        log_rest = log_beta - z
        if diagonal:
            log_rest = jnp.where(causal, log_rest, 0.0)
        hi = log_rest.astype(BF16)
        lo = (log_rest - hi.astype(F32)).astype(BF16)
        later = _dot(hi, upper) + _dot(lo, upper)
        row_sum = jnp.sum(log_rest, axis=-1, keepdims=True)
        yield
        if diagonal:
            w = jnp.where(causal, jnp.exp(log_beta + later), 0.0)
            acc_s[pl.ds(q0, t), :] = _dot(w.astype(BF16), v_ref[pl.ds(k0, t), :])
            carry_s[pl.ds(q0, t), :] = row_sum
        else:
            carry = carry_s[pl.ds(q0, t), :]
            w = jnp.exp(log_beta + (later + carry))
            acc_s[pl.ds(q0, t), :] += _dot(w.astype(BF16), v_ref[pl.ds(k0, t), :])
            carry_s[pl.ds(q0, t), :] = carry + row_sum

    first = [tile(i * t, i * t, True) for i in range(nq)]
    second = [tile(i * t, (i - 1) * t, False) for i in range(1, nq)]
    _emit_staggered(first + second, 1, SB_SKEW)

    rows = _iota2((s, 1), 0)

    def sweep(d, _):
        live = jnp.max(jnp.where(rows >= d * t, carry_s[...], SB_DONE)) > SB_DONE

        @pl.when(live)
        def _():
            def body(i, _):
                for _ in tile(pl.multiple_of(i * t, t), pl.multiple_of((i - d) * t, t), False):
                    pass
                return 0

            lax.fori_loop(d, nq, body, 0)

        return 0

    lax.fori_loop(2, nq, sweep, 0)
    o_ref[...] = acc_s[...].astype(o_ref.dtype)


def _stickbreaking(proj, q_g, k_g, batch, heads, q_off, k_off, v_off):
    m = proj.shape[0]
    s = m // batch
    qb, kb, vb = q_off // HEAD_DIM, k_off // HEAD_DIM, v_off // HEAD_DIM
    return pl.pallas_call(
        _sb_kernel,
        grid=(batch, heads),
        in_specs=[
            pl.BlockSpec((s, HEAD_DIM), lambda b, h: (b, qb + h)),
            pl.BlockSpec((s, HEAD_DIM), lambda b, h: (b, kb + h)),
            pl.BlockSpec((s, HEAD_DIM), lambda b, h: (b, vb + h)),
            pl.BlockSpec((1, HEAD_DIM), lambda b, h: (0, 0)),
            pl.BlockSpec((1, HEAD_DIM), lambda b, h: (0, 0)),
        ],
        out_specs=pl.BlockSpec((s, HEAD_DIM), lambda b, h: (b, h)),
        out_shape=jax.ShapeDtypeStruct((m, heads * HEAD_DIM), BF16),
        scratch_shapes=[
            pltpu.VMEM((s, HEAD_DIM), BF16),
            pltpu.VMEM((s, HEAD_DIM), BF16),
            pltpu.VMEM((s, HEAD_DIM), F32),
            pltpu.VMEM((s, 1), F32),
        ],
        compiler_params=_cparams(("parallel", "parallel")),
        name="stickbreaking_attn",
    )(proj, proj, proj, q_g.reshape(1, HEAD_DIM), k_g.reshape(1, HEAD_DIM))


def kernel(x, mem, norm_g, mem_norm_g, mem_w_kv, xa_q_norm_g, xa_k_norm_g, w_out,
           dn_w_in, dn_conv_w, dn_a_log, dn_dt_bias, dn_out_norm_g,
           sb_w_in, sb_q_norm_g, sb_k_norm_g):
    batch, seq, d = x.shape
    depth = norm_g.shape[0]
    inner = w_out.shape[1]
    mix_width = inner - XA_WIDTH
    heads = mix_width // HEAD_DIM
    m = batch * seq
    tn_dn = 1280

    dn_qk_width = (heads // 2) * HEAD_DIM
    dn_ab_off = 2 * dn_qk_width + mix_width
    dn_mix_cols = dn_ab_off + 2 * heads
    dn_mix_pad = -(-(dn_ab_off + 2 * LANES) // tn_dn) * tn_dn
    sb_mix_cols = 3 * mix_width

    kv = _memkv(mem.reshape(-1, d), mem_norm_g, mem_w_kv, xa_k_norm_g)
    w_out_b = w_out.astype(BF16)

    x2 = x.reshape(m, d)
    h = None
    for i in range(depth):
        j = i // 2
        src, g, normalise = (x2, norm_g[i], True) if h is None else (h, norm_g[i], False)
        if i % 2 == 0:
            w_side = dn_w_in[j][:, dn_mix_cols:]
            pmix = _inproj(src, g, dn_w_in[j], dn_mix_pad, 1024, tn_dn, normalise)
            side = _inproj(src, g, w_side, w_side.shape[1], 1024, tn_dn, normalise)
            xq_off = 0
            gates = _dn_gates(pmix, dn_a_log[j], dn_dt_bias[j], batch, dn_ab_off)
            mix = _deltanet(pmix, dn_conv_w[j], gates, dn_out_norm_g[j], batch,
                            0, dn_qk_width, 2 * dn_qk_width)
        else:
            side = _inproj(src, g, sb_w_in[j], sb_w_in.shape[2], 2048, 1024, normalise)
            xq_off = sb_mix_cols
            mix = _stickbreaking(side, sb_q_norm_g[j], sb_k_norm_g[j], batch, heads,
                                 0, mix_width, 2 * mix_width)
        xa = _xattn(side, kv, xa_q_norm_g[i], i, batch, xq_off)
        next_g = norm_g[i + 1] if i + 1 < depth else None
        res = _outproj(side, xq_off + XA_WIDTH, mix, xa, x2, w_out_b, i, next_g)
        x2, h = res if next_g is not None else (res, None)
    return x2.reshape(batch, seq, d)
```

```python
import functools

import jax
import jax.numpy as jnp
from jax import lax
from jax.experimental import pallas as pl
from jax.experimental.pallas import tpu as pltpu

F32 = jnp.float32
BF16 = jnp.bfloat16
EPS = 1e-6

XA_HEADS = 4
XA_DIM = 256
XA_WIDTH = XA_HEADS * XA_DIM
HEAD_DIM = 128
DN_CONV = 4
DN_CHUNK = 64
DN_PAIR = 2 * DN_CHUNK
N_GATE_ROWS = 8

LANES = 128
VMEM_BYTES_V7X = 64 * 1024 * 1024
VMEM_LIMIT = VMEM_BYTES_V7X - 8 * 1024 * 1024

NEG_BIG = -1e30


def _cparams(sem):
    return pltpu.CompilerParams(dimension_semantics=sem, vmem_limit_bytes=VMEM_LIMIT)


def _dot(a, b):
    return jnp.dot(a, b, preferred_element_type=F32)


def _dot_nt(a, b):
    return lax.dot_general(a, b, (((1,), (1,)), ((), ())), preferred_element_type=F32)


def _split3(x):
    hi = x.astype(BF16)
    r = x - hi.astype(F32)
    mid = r.astype(BF16)
    lo = (r - mid.astype(F32)).astype(BF16)
    return hi, mid, lo


def _sigmoid(x):
    return 1.0 / (1.0 + jnp.exp(-x))


def _softplus(x):
    return jnp.maximum(x, 0.0) + jnp.log(1.0 + jnp.exp(-jnp.abs(x)))


def _iota2(shape, dim):
    return lax.broadcasted_iota(jnp.int32, shape, dim)


def _emit_staggered(items, lag, skew=1):
    live = list(enumerate(items))
    stage = {n: 0 for n, _ in live}
    t = 0
    while live:
        due = [(n, g) for n, g in live if n * lag + stage[n] * skew <= t]
        for n, g in sorted(due, key=lambda ng: -stage[ng[0]]):
            try:
                next(g)
                stage[n] += 1
            except StopIteration:
                live.remove((n, g))
        t += 1


def _rms_rows(x, g):
    ms = jnp.mean(x * x, axis=-1, keepdims=True)
    return (x * lax.rsqrt(ms + EPS)) * g


def _inproj_kernel(x_ref, g_ref, w_ref, o_ref, wb_s, *, normalise, w_transposed):
    @pl.when(pl.program_id(1) == 0)
    def _():
        wb_s[...] = w_ref[...].astype(BF16)

    if normalise:
        h = _rms_rows(x_ref[...], g_ref[...]).astype(BF16)
    else:
        h = x_ref[...]
    acc = _dot_nt(h, wb_s[...]) if w_transposed else _dot(h, wb_s[...])
    o_ref[...] = acc.astype(o_ref.dtype)


def _inproj(x2, g, w, cols, tm, tn, normalise, w_transposed=False, col0=0):
    m, d = x2.shape
    if w_transposed:
        w_block = (tn, d)
        w_spec = pl.BlockSpec((pl.Element(tn), pl.Element(d)),
                              lambda j, i: (pl.multiple_of(col0 + j * tn, 8), 0))
    else:
        assert col0 == 0
        w_block, w_spec = (d, tn), pl.BlockSpec((d, tn), lambda j, i: (0, j))
    return pl.pallas_call(
        functools.partial(_inproj_kernel, normalise=normalise, w_transposed=w_transposed),
        grid=(cols // tn, m // tm),
        in_specs=[
            pl.BlockSpec((tm, d), lambda j, i: (i, 0)),
            pl.BlockSpec((1, d), lambda j, i: (0, 0)),
            w_spec,
        ],
        out_specs=pl.BlockSpec((tm, tn), lambda j, i: (i, j)),
        out_shape=jax.ShapeDtypeStruct((m, cols), BF16),
        scratch_shapes=[pltpu.VMEM(w_block, BF16)],
        compiler_params=_cparams(("arbitrary", "arbitrary")),
        name="rmsnorm_inproj",
    )(x2, g.reshape(1, d), w)


def _memkv_kernel(mem_ref, g_ref, w_ref, kg_ref, o_ref, mn_s, *, k_tiles):
    @pl.when((pl.program_id(0) == 0) & (pl.program_id(1) == 0))
    def _():
        mn_s[...] = _rms_rows(mem_ref[...], g_ref[...]).astype(BF16)

    kv = _dot(mn_s[...], w_ref[0].astype(BF16))
    is_key = pl.program_id(1) < k_tiles

    @pl.when(is_key)
    def _():
        for c in range(kv.shape[1] // XA_DIM):
            kh = kv[:, c * XA_DIM:(c + 1) * XA_DIM]
            hs = jnp.mean(kh * kh, axis=-1, keepdims=True)
            o_ref[0, :, c * XA_DIM:(c + 1) * XA_DIM] = (
                (kh * lax.rsqrt(hs + EPS)) * kg_ref[0]).astype(o_ref.dtype)

    @pl.when(jnp.logical_not(is_key))
    def _():
        o_ref[0] = kv.astype(o_ref.dtype)


def _memkv(mem2, mem_g, w_kv, k_g, tn=512):
    depth, d, e = w_kv.shape
    mm = mem2.shape[0]
    return pl.pallas_call(
        functools.partial(_memkv_kernel, k_tiles=XA_WIDTH // tn),
        grid=(depth, e // tn),
        in_specs=[
            pl.BlockSpec((mm, d), lambda l, j: (0, 0)),
            pl.BlockSpec((1, d), lambda l, j: (0, 0)),
            pl.BlockSpec((1, d, tn), lambda l, j: (l, 0, j)),
            pl.BlockSpec((1, 1, XA_DIM), lambda l, j: (l, 0, 0)),
        ],
        out_specs=pl.BlockSpec((1, mm, tn), lambda l, j: (l, 0, j)),
        out_shape=jax.ShapeDtypeStruct((depth, mm, e), BF16),
        scratch_shapes=[pltpu.VMEM((mm, d), BF16)],
        compiler_params=_cparams(("arbitrary", "arbitrary")),
        name="mem_kv",
    )(mem2, mem_g.reshape(1, d), w_kv, k_g.reshape(depth, 1, XA_DIM))


def _xattn_kernel(q_ref, qg_ref, k_ref, v_ref, o_ref):
    for h in range(XA_HEADS):
        cs = slice(h * XA_DIM, (h + 1) * XA_DIM)
        q = q_ref[:, cs].astype(F32)
        ms = jnp.mean(q * q, axis=-1, keepdims=True)
        qn = ((q * lax.rsqrt(ms + EPS)) * qg_ref[...]).astype(BF16)
        s = _dot_nt(qn, k_ref[0, :, cs]) * (XA_DIM ** -0.5)
        e = jnp.exp(s - jnp.max(s, axis=-1, keepdims=True))
        p = e / jnp.sum(e, axis=-1, keepdims=True)
        o_ref[:, cs] = _dot(p.astype(BF16), v_ref[0, :, cs]).astype(o_ref.dtype)


def _xattn(proj, kv, q_g, layer, batch, xq_off, tm=512):
    m = proj.shape[0]
    n_mem = kv.shape[1] // batch
    nt = m // batch // tm
    qb = xq_off // XA_WIDTH
    return pl.pallas_call(
        _xattn_kernel,
        grid=(batch, nt),
        in_specs=[
            pl.BlockSpec((tm, XA_WIDTH), lambda b, i: (b * nt + i, qb)),
            pl.BlockSpec((1, XA_DIM), lambda b, i: (0, 0)),
            pl.BlockSpec((1, n_mem, XA_WIDTH), lambda b, i: (layer, b, 0)),
            pl.BlockSpec((1, n_mem, XA_WIDTH), lambda b, i: (layer, b, 1)),
        ],
        out_specs=pl.BlockSpec((tm, XA_WIDTH), lambda b, i: (b * nt + i, 0)),
        out_shape=jax.ShapeDtypeStruct((m, XA_WIDTH), BF16),
        compiler_params=_cparams(("parallel", "parallel")),
        name="mem_xattn",
    )(proj, q_g.reshape(1, XA_DIM), kv, kv)


def _outproj_kernel(*refs, emit_next):
    z_refs = refs[:4]
    mix_ref, xa_ref, x_ref, w_ref = refs[4:8]
    if emit_next:
        gn_ref, o_ref, hn_ref, y_s = refs[8:]
    else:
        o_ref, y_s = refs[8:]
    zw = z_refs[0].shape[1]
    n_mix = mix_ref.shape[1] // zw
    for c, z_ref in enumerate(z_refs):
        z = z_ref[...].astype(F32)
        if c < n_mix:
            br = mix_ref[:, c * zw:(c + 1) * zw]
        else:
            br = xa_ref[:, (c - n_mix) * zw:(c - n_mix + 1) * zw]
        y_s[:, c * zw:(c + 1) * zw] = (br.astype(F32) * (z * _sigmoid(z))).astype(BF16)
    out = x_ref[...] + _dot(y_s[...], w_ref[...])
    o_ref[...] = out
    if emit_next:
        hn_ref[...] = _rms_rows(out, gn_ref[...]).astype(BF16)


def _outproj(proj, z_off, mix, xa, x2, w, layer, next_g, tm=256):
    m, d = x2.shape
    inner = w.shape[1]
    mw = mix.shape[1]
    zw = inner // 4
    zb = z_off // zw
    emit_next = next_g is not None
    row_spec = pl.BlockSpec((tm, d), lambda i: (i, 0))
    in_specs = [pl.BlockSpec((tm, zw), functools.partial(lambda c, i: (i, zb + c), c))
                for c in range(4)]
    in_specs += [
        pl.BlockSpec((tm, mw), lambda i: (i, 0)),
        pl.BlockSpec((tm, inner - mw), lambda i: (i, 0)),
        row_spec,
        pl.BlockSpec((None, inner, d), lambda i: (layer, 0, 0), pipeline_mode=pl.Buffered(1)),
    ]
    operands = [proj, proj, proj, proj, mix, xa, x2, w]
    out_specs, out_shape = row_spec, jax.ShapeDtypeStruct((m, d), F32)
    if emit_next:
        in_specs.append(pl.BlockSpec((1, d), lambda i: (0, 0)))
        operands.append(next_g.reshape(1, d))
        out_specs = (row_spec, row_spec)
        out_shape = (out_shape, jax.ShapeDtypeStruct((m, d), BF16))
    return pl.pallas_call(
        functools.partial(_outproj_kernel, emit_next=emit_next),
        grid=(m // tm,),
        in_specs=in_specs,
        out_specs=out_specs,
        out_shape=out_shape,
        scratch_shapes=[pltpu.VMEM((tm, inner), BF16)],
        compiler_params=_cparams(("parallel",)),
        name="gate_outproj",
    )(*operands)


def _dn_gates_kernel(ab_ref, alog_ref, dtb_ref, o_ref, *, heads):
    s = ab_ref.shape[0]
    t = ab_ref[:, :LANES].astype(F32).T
    a = t[0:heads]
    b = t[heads:2 * heads]
    g = -jnp.exp(alog_ref[...]) * _softplus(a + dtb_ref[...])
    beta = _sigmoid(b)

    ii = _iota2((DN_PAIR, DN_PAIR), 0)
    jj = _iota2((DN_PAIR, DN_PAIR), 1)
    lo_i = ii < DN_CHUNK
    lo_j = jj < DN_CHUNK
    same = lo_i == lo_j
    one = lambda m: jnp.where(m, 1.0, 0.0).astype(BF16)
    rhs = jnp.concatenate(
        [one(same & (ii <= jj)), one(same), one(lo_i), one(jnp.logical_not(lo_i))], axis=1)

    for p in range(s // DN_PAIR):
        ls = slice(p * DN_PAIR, (p + 1) * DN_PAIR)
        hi, mid, lo = _split3(g[:, ls])
        r = _dot(hi, rhs) + _dot(mid, rhs) + _dot(lo, rhs)
        gc = r[:, 0:DN_PAIR]
        rows = (beta[:, ls], gc, r[:, DN_PAIR:2 * DN_PAIR] - gc,
                jnp.exp(r[:, 2 * DN_PAIR:3 * DN_PAIR]), jnp.exp(r[:, 3 * DN_PAIR:4 * DN_PAIR]))
        for n, val in enumerate(rows):
            o_ref[:, n, ls] = val
        for n in range(len(rows), N_GATE_ROWS):
            o_ref[:, n, ls] = jnp.zeros_like(gc)


def _dn_gates(proj, a_log, dt_bias, batch, ab_off):
    m = proj.shape[0]
    s = m // batch
    heads = a_log.shape[0]
    blk = 2 * LANES
    return pl.pallas_call(
        functools.partial(_dn_gates_kernel, heads=heads),
        grid=(batch,),
        in_specs=[
            pl.BlockSpec((s, blk), lambda b: (b, ab_off // blk)),
            pl.BlockSpec((heads, 1), lambda b: (0, 0)),
            pl.BlockSpec((heads, 1), lambda b: (0, 0)),
        ],
        out_specs=pl.BlockSpec((None, heads, N_GATE_ROWS, s), lambda b: (b, 0, 0, 0)),
        out_shape=jax.ShapeDtypeStruct((batch, heads, N_GATE_ROWS, s), F32),
        compiler_params=_cparams(("parallel",)),
        name="dn_gates",
    )(proj, a_log.reshape(heads, 1), dt_bias.reshape(heads, 1))


CONV_ROWS = 256


def _conv_silu_block(x_ref, w_ref, dst_ref, l2_scale, i):
    width = x_ref.shape[1]
    w = w_ref[...]
    row8 = _iota2((8, width), 0)
    r0 = pl.multiple_of(i * CONV_ROWS, CONV_ROWS)
    x = x_ref[pl.ds(r0, CONV_ROWS), :].astype(F32)
    p0 = pl.multiple_of(jnp.maximum(r0 - 16, 0), 16)
    prev = x_ref[pl.ds(p0, 16), :].astype(F32)[8:16]
    prev = prev * jnp.where(i > 0, 1.0, 0.0)
    acc = x * w[DN_CONV - 1:DN_CONV]
    for k in range(1, DN_CONV):
        xr = pltpu.roll(x, k, axis=0)
        pr = pltpu.roll(prev, k, axis=0)
        head = jnp.where(row8 < k, pr, xr[0:8])
        xr = jnp.concatenate([head, xr[8:]], axis=0)
        acc = acc + xr * w[DN_CONV - 1 - k:DN_CONV - k]
    y = acc * _sigmoid(acc)
    if l2_scale is not None:
        for c in range(width // HEAD_DIM):
            cs = slice(c * HEAD_DIM, (c + 1) * HEAD_DIM)
            yc = y[:, cs]
            ss = jnp.sum(yc * yc, axis=-1, keepdims=True)
            dst_ref[pl.ds(r0, CONV_ROWS), cs] = (yc * lax.rsqrt(ss + EPS)) * l2_scale
    else:
        dst_ref[pl.ds(r0, CONV_ROWS), :] = y


DN_QK_PER_STEP = 2
DN_LOCAL_PAIRS = 2


def _dn_kernel(q_ref, k_ref, v_ref, cwq_ref, cwk_ref, cwv_ref, gates_ref, og_ref, o_ref,
               qn_s, kn_s, vn_s, u0_s, wq_s, qkd_s, kdt_s, st_s):
    n_vh = 2 * DN_QK_PER_STEP
    head0 = n_vh * pl.program_id(1)
    s = q_ref.shape[0]
    n_pairs = s // DN_PAIR

    def conv_silu(x_ref, w_ref, dst_ref, l2_scale):
        def block(i, carry):
            _conv_silu_block(x_ref, w_ref, dst_ref, l2_scale, i)
            return carry

        lax.fori_loop(0, s // CONV_ROWS, block, 0)

    conv_silu(q_ref, cwq_ref, qn_s, HEAD_DIM ** -0.5)
    conv_silu(k_ref, cwk_ref, kn_s, 1.0)
    conv_silu(v_ref, cwv_ref, vn_s, None)

    ii = _iota2((DN_PAIR, DN_PAIR), 0)
    jj = _iota2((DN_PAIR, DN_PAIR), 1)
    eye = ii == jj
    incl = ((ii < DN_CHUNK) == (jj < DN_CHUNK)) & (ii >= jj)

    def gate_row(r, vh, r0):
        return gates_ref[head0 + vh, pl.ds(r, 1), pl.ds(r0, DN_PAIR)]

    def col(x):
        return jnp.broadcast_to(x, (DN_PAIR, DN_PAIR)).T

    def mm(a, b):
        return _dot(a.astype(BF16), b.astype(BF16))

    def hs(n):
        return slice(n * HEAD_DIM, (n + 1) * HEAD_DIM)

    def local_chain(p, qh, e, q, k, v, qk, kk):
        r0 = pl.multiple_of(p * DN_PAIR, DN_PAIR)
        vh = 2 * qh + e
        gc_r = gate_row(1, vh, r0)
        beta_c = col(gate_row(0, vh, r0))
        gc_c = col(gc_r)
        gl_c = col(gate_row(2, vh, r0))
        decay = jnp.exp(jnp.where(incl, gc_c - gc_r, NEG_BIG))
        a = jnp.where(eye, 0.0, beta_c * kk * decay)
        x = jnp.where(eye, 1.0, -a)
        pw = mm(a, a)
        yield
        for _ in range(4):
            x, pw = x + mm(x, pw), mm(pw, pw)
            yield
        x = x + mm(x, pw)
        ep_c = jnp.exp(gc_c)
        rhs = jnp.concatenate([v * beta_c, k * (beta_c * ep_c)], axis=1)
        yield
        sol = mm(x, rhs)
        qd = q * ep_c
        kdt = (k * jnp.exp(gl_c)).T
        qkd = qk * decay
        yield
        u0 = sol[:, :HEAD_DIM]
        w = sol[:, HEAD_DIM:]
        for cc in range(2):
            c = 2 * p + cc
            rs = slice(cc * DN_CHUNK, (cc + 1) * DN_CHUNK)
            u0_s[vh, c] = u0[rs]
            wq_s[vh, c] = jnp.concatenate([w[rs], qd[rs]], axis=0).astype(BF16)
            qkd_s[vh, c] = qkd[rs, rs].astype(BF16)
            kdt_s[vh, c] = kdt[:, rs].astype(BF16)

    def local(it, _):
        chains = []
        for pp in range(DN_LOCAL_PAIRS):
            p = DN_LOCAL_PAIRS * it + pp
            r0 = pl.multiple_of(p * DN_PAIR, DN_PAIR)
            for qh in range(DN_QK_PER_STEP):
                q = qn_s[pl.ds(r0, DN_PAIR), hs(qh)]
                k = kn_s[pl.ds(r0, DN_PAIR), hs(qh)]
                qk2 = _dot_nt(jnp.concatenate([q, k], axis=0).astype(BF16), k.astype(BF16))
                for e in range(2):
                    v = vn_s[pl.ds(r0, DN_PAIR), hs(2 * qh + e)]
                    chains.append(local_chain(p, qh, e, q, k, v, qk2[:DN_PAIR], qk2[DN_PAIR:]))
        _emit_staggered(chains, 0)
        return 0

    lax.fori_loop(0, n_pairs // DN_LOCAL_PAIRS, local, 0)

    og = og_ref[...]

    def recur_head(vh, p, cc):
        r0 = pl.multiple_of(p * DN_PAIR, DN_PAIR)
        c = 2 * p + cc
        row0 = pl.multiple_of(c * DN_CHUNK, DN_CHUNK)
        r = _dot(wq_s[vh, c], st_s[vh].astype(BF16))
        yield
        u = (u0_s[vh, c] - r[:DN_CHUNK]).astype(BF16)
        o = r[DN_CHUNK:] + _dot(qkd_s[vh, c], u)
        st_s[vh] = st_s[vh] * gate_row(3 + cc, vh, r0) + _dot(kdt_s[vh, c], u)
        yield
        ms = jnp.mean(o * o, axis=-1, keepdims=True)
        o_ref[pl.ds(row0, DN_CHUNK), hs(vh)] = ((o * lax.rsqrt(ms + EPS)) * og).astype(o_ref.dtype)

    def recur(p, _):
        for cc in range(2):
            _emit_staggered([recur_head(vh, p, cc) for vh in range(n_vh)], 0)
        return 0

    st_s[...] = jnp.zeros_like(st_s)
    lax.fori_loop(0, n_pairs, recur, 0)


def _deltanet(proj, conv_w, gates, out_g, batch, q_off, k_off, v_off):
    m = proj.shape[0]
    s = m // batch
    heads = gates.shape[1]
    n_vh = 2 * DN_QK_PER_STEP
    steps = heads // n_vh
    n_chunks = s // DN_CHUNK
    qw = DN_QK_PER_STEP * HEAD_DIM
    vw = n_vh * HEAD_DIM
    qb, kb, vb = q_off // qw, k_off // qw, v_off // vw
    ck = (heads // 2) * HEAD_DIM // qw
    cv = heads * HEAD_DIM // vw
    return pl.pallas_call(
        _dn_kernel,
        grid=(batch, steps),
        in_specs=[
            pl.BlockSpec((s, qw), lambda b, h: (b, qb + h)),
            pl.BlockSpec((s, qw), lambda b, h: (b, kb + h)),
            pl.BlockSpec((s, vw), lambda b, h: (b, vb + h)),
            pl.BlockSpec((DN_CONV, qw), lambda b, h: (0, h)),
            pl.BlockSpec((DN_CONV, qw), lambda b, h: (0, ck + h)),
            pl.BlockSpec((DN_CONV, vw), lambda b, h: (0, cv + h)),
            pl.BlockSpec((None, heads, N_GATE_ROWS, s), lambda b, h: (b, 0, 0, 0)),
            pl.BlockSpec((1, HEAD_DIM), lambda b, h: (0, 0)),
        ],
        out_specs=pl.BlockSpec((s, vw), lambda b, h: (b, h)),
        out_shape=jax.ShapeDtypeStruct((m, heads * HEAD_DIM), BF16),
        scratch_shapes=[
            pltpu.VMEM((s, qw), F32),
            pltpu.VMEM((s, qw), F32),
            pltpu.VMEM((s, vw), F32),
            pltpu.VMEM((n_vh, n_chunks, DN_CHUNK, HEAD_DIM), F32),
            pltpu.VMEM((n_vh, n_chunks, 2 * DN_CHUNK, HEAD_DIM), BF16),
            pltpu.VMEM((n_vh, n_chunks, DN_CHUNK, DN_CHUNK), BF16),
            pltpu.VMEM((n_vh, n_chunks, HEAD_DIM, DN_CHUNK), BF16),
            pltpu.VMEM((n_vh, HEAD_DIM, HEAD_DIM), F32),
        ],
        compiler_params=_cparams(("parallel", "arbitrary")),
        name="gated_deltanet",
    )(proj, proj, proj, conv_w, conv_w, conv_w, gates, out_g.reshape(1, HEAD_DIM))


SB_TILE = 256


SB_DONE = -110.0
SB_SKEW = 3


def _sb_kernel(q_ref, k_ref, v_ref, qg_ref, kg_ref, o_ref, qn_s, kn_s, acc_s, carry_s):
    s = q_ref.shape[0]
    t = SB_TILE
    nq = s // t

    def normalise(x_ref, g_ref, scale, dst):
        x = x_ref[...].astype(F32)
        ms = jnp.mean(x * x, axis=-1, keepdims=True)
        dst[...] = ((x * lax.rsqrt(ms + EPS)) * g_ref[...] * scale).astype(BF16)

    normalise(q_ref, qg_ref, HEAD_DIM ** -0.5, qn_s)
    normalise(k_ref, kg_ref, 1.0, kn_s)

    ii = _iota2((t, t), 0)
    jj = _iota2((t, t), 1)
    upper = jnp.where(ii > jj, 1.0, 0.0).astype(BF16)
    causal = jj < ii

    def tile(q0, k0, diagonal):
        z = _dot_nt(qn_s[pl.ds(q0, t), :], kn_s[pl.ds(k0, t), :])
        yield
        log_beta = jnp.minimum(z, 0.0) - jnp.log(1.0 + jnp.exp(-jnp.abs(z)))
        log_rest = log_beta - z
        if diagonal:
            log_rest = jnp.where(causal, log_rest, 0.0)
        hi = log_rest.astype(BF16)
        lo = (log_rest - hi.astype(F32)).astype(BF16)
        later = _dot(hi, upper) + _dot(lo, upper)
        row_sum = jnp.sum(log_rest, axis=-1, keepdims=True)
        yield
        if diagonal:
            w = jnp.where(causal, jnp.exp(log_beta + later), 0.0)
            acc_s[pl.ds(q0, t), :] = _dot(w.astype(BF16), v_ref[pl.ds(k0, t), :])
            carry_s[pl.ds(q0, t), :] = row_sum
        else:
            carry = carry_s[pl.ds(q0, t), :]
            w = jnp.exp(log_beta + (later + carry))
            acc_s[pl.ds(q0, t), :] += _dot(w.astype(BF16), v_ref[pl.ds(k0, t), :])
            carry_s[pl.ds(q0, t), :] = carry + row_sum

    first = [tile(i * t, i * t, True) for i in range(nq)]
    second = [tile(i * t, (i - 1) * t, False) for i in range(1, nq)]
    _emit_staggered(first + second, 1, SB_SKEW)

    rows = _iota2((s, 1), 0)

    def sweep(d, _):
        live = jnp.max(jnp.where(rows >= d * t, carry_s[...], SB_DONE)) > SB_DONE

        @pl.when(live)
        def _():
            def body(i, _):
                for _ in tile(pl.multiple_of(i * t, t), pl.multiple_of((i - d) * t, t), False):
                    pass
                return 0

            lax.fori_loop(d, nq, body, 0)

        return 0

    lax.fori_loop(2, nq, sweep, 0)
    o_ref[...] = acc_s[...].astype(o_ref.dtype)


def _stickbreaking(proj, q_g, k_g, batch, heads, q_off, k_off, v_off):
    m = proj.shape[0]
    s = m // batch
    qb, kb, vb = q_off // HEAD_DIM, k_off // HEAD_DIM, v_off // HEAD_DIM
    return pl.pallas_call(
        _sb_kernel,
        grid=(batch, heads),
        in_specs=[
            pl.BlockSpec((s, HEAD_DIM), lambda b, h: (b, qb + h)),
            pl.BlockSpec((s, HEAD_DIM), lambda b, h: (b, kb + h)),
            pl.BlockSpec((s, HEAD_DIM), lambda b, h: (b, vb + h)),
            pl.BlockSpec((1, HEAD_DIM), lambda b, h: (0, 0)),
            pl.BlockSpec((1, HEAD_DIM), lambda b, h: (0, 0)),
        ],
        out_specs=pl.BlockSpec((s, HEAD_DIM), lambda b, h: (b, h)),
        out_shape=jax.ShapeDtypeStruct((m, heads * HEAD_DIM), BF16),
        scratch_shapes=[
            pltpu.VMEM((s, HEAD_DIM), BF16),
            pltpu.VMEM((s, HEAD_DIM), BF16),
            pltpu.VMEM((s, HEAD_DIM), F32),
            pltpu.VMEM((s, 1), F32),
        ],
        compiler_params=_cparams(("parallel", "parallel")),
        name="stickbreaking_attn",
    )(proj, proj, proj, q_g.reshape(1, HEAD_DIM), k_g.reshape(1, HEAD_DIM))


def kernel(x, mem, norm_g, mem_norm_g, mem_w_kv, xa_q_norm_g, xa_k_norm_g, w_out,
           dn_w_in, dn_conv_w, dn_a_log, dn_dt_bias, dn_out_norm_g,
           sb_w_in, sb_q_norm_g, sb_k_norm_g):
    batch, seq, d = x.shape
    depth = norm_g.shape[0]
    inner = w_out.shape[1]
    mix_width = inner - XA_WIDTH
    heads = mix_width // HEAD_DIM
    m = batch * seq
    tn_dn = 1280

    dn_qk_width = (heads // 2) * HEAD_DIM
    dn_ab_off = 2 * dn_qk_width + mix_width
    dn_mix_cols = dn_ab_off + 2 * heads
    dn_mix_pad = -(-(dn_ab_off + 2 * LANES) // tn_dn) * tn_dn
    sb_mix_cols = 3 * mix_width

    kv = _memkv(mem.reshape(-1, d), mem_norm_g, mem_w_kv, xa_k_norm_g)
    w_out_b = w_out.astype(BF16)

    x2 = x.reshape(m, d)
    h = None
    for i in range(depth):
        j = i // 2
        src, g, normalise = (x2, norm_g[i], True) if h is None else (h, norm_g[i], False)
        if i % 2 == 0:
            w_t = jnp.swapaxes(dn_w_in[j], 0, 1)
            pmix = _inproj(src, g, w_t, dn_mix_pad, 1024, tn_dn, normalise, True)
            side = _inproj(src, g, w_t, w_t.shape[0] - dn_mix_cols, 1024, tn_dn, normalise, True,
                           col0=dn_mix_cols)
            xq_off = 0
            gates = _dn_gates(pmix, dn_a_log[j], dn_dt_bias[j], batch, dn_ab_off)
            mix = _deltanet(pmix, dn_conv_w[j], gates, dn_out_norm_g[j], batch,
                            0, dn_qk_width, 2 * dn_qk_width)
        else:
            side = _inproj(src, g, sb_w_in[j], sb_w_in.shape[2], 2048, 1024, normalise)
            xq_off = sb_mix_cols
            mix = _stickbreaking(side, sb_q_norm_g[j], sb_k_norm_g[j], batch, heads,
                                 0, mix_width, 2 * mix_width)
        xa = _xattn(side, kv, xa_q_norm_g[i], i, batch, xq_off)
        next_g = norm_g[i + 1] if i + 1 < depth else None
        res = _outproj(side, xq_off + XA_WIDTH, mix, xa, x2, w_out_b, i, next_g)
        x2, h = res if next_g is not None else (res, None)
    return x2.reshape(batch, seq, d)
```

```python
import functools

import jax
import jax.numpy as jnp
from jax import lax
from jax.experimental import pallas as pl
from jax.experimental.pallas import tpu as pltpu

F32 = jnp.float32
BF16 = jnp.bfloat16
EPS = 1e-6

XA_HEADS = 4
XA_DIM = 256
XA_WIDTH = XA_HEADS * XA_DIM
HEAD_DIM = 128
DN_CONV = 4
DN_CHUNK = 64
DN_PAIR = 2 * DN_CHUNK
N_GATE_ROWS = 8

LANES = 128
VMEM_BYTES_V7X = 64 * 1024 * 1024
VMEM_LIMIT = VMEM_BYTES_V7X - 8 * 1024 * 1024

NEG_BIG = -1e30


def _cparams(sem):
    return pltpu.CompilerParams(dimension_semantics=sem, vmem_limit_bytes=VMEM_LIMIT)


def _dot(a, b):
    return jnp.dot(a, b, preferred_element_type=F32)


def _dot_nt(a, b):
    return lax.dot_general(a, b, (((1,), (1,)), ((), ())), preferred_element_type=F32)


def _split3(x):
    hi = x.astype(BF16)
    r = x - hi.astype(F32)
    mid = r.astype(BF16)
    lo = (r - mid.astype(F32)).astype(BF16)
    return hi, mid, lo


def _sigmoid(x):
    return 1.0 / (1.0 + jnp.exp(-x))


def _softplus(x):
    return jnp.maximum(x, 0.0) + jnp.log(1.0 + jnp.exp(-jnp.abs(x)))


def _iota2(shape, dim):
    return lax.broadcasted_iota(jnp.int32, shape, dim)


def _emit_staggered(items, lag, skew=1):
    live = list(enumerate(items))
    stage = {n: 0 for n, _ in live}
    t = 0
    while live:
        due = [(n, g) for n, g in live if n * lag + stage[n] * skew <= t]
        for n, g in sorted(due, key=lambda ng: -stage[ng[0]]):
            try:
                next(g)
                stage[n] += 1
            except StopIteration:
                live.remove((n, g))
        t += 1


def _rms_rows(x, g):
    ms = jnp.mean(x * x, axis=-1, keepdims=True)
    return (x * lax.rsqrt(ms + EPS)) * g


def _inproj_kernel(x_ref, g_ref, w_ref, o_ref, wb_s, *, normalise, w_transposed):
    @pl.when(pl.program_id(1) == 0)
    def _():
        wb_s[...] = w_ref[...].astype(BF16)

    if normalise:
        h = _rms_rows(x_ref[...], g_ref[...]).astype(BF16)
    else:
        h = x_ref[...]
    acc = _dot_nt(h, wb_s[...]) if w_transposed else _dot(h, wb_s[...])
    o_ref[...] = acc.astype(o_ref.dtype)


def _inproj(x2, g, w, cols, tm, tn, normalise, w_transposed=False, col0=0):
    m, d = x2.shape
    if w_transposed:
        w_block = (tn, d)
        w_spec = pl.BlockSpec((pl.Element(tn), pl.Element(d)),
                              lambda j, i: (pl.multiple_of(col0 + j * tn, 8), 0))
    else:
        assert col0 == 0
        w_block, w_spec = (d, tn), pl.BlockSpec((d, tn), lambda j, i: (0, j))
    return pl.pallas_call(
        functools.partial(_inproj_kernel, normalise=normalise, w_transposed=w_transposed),
        grid=(cols // tn, m // tm),
        in_specs=[
            pl.BlockSpec((tm, d), lambda j, i: (i, 0)),
            pl.BlockSpec((1, d), lambda j, i: (0, 0)),
            w_spec,
        ],
        out_specs=pl.BlockSpec((tm, tn), lambda j, i: (i, j)),
        out_shape=jax.ShapeDtypeStruct((m, cols), BF16),
        scratch_shapes=[pltpu.VMEM(w_block, BF16)],
        compiler_params=_cparams(("arbitrary", "arbitrary")),
        name="rmsnorm_inproj",
    )(x2, g.reshape(1, d), w)


def _memkv_kernel(mem_ref, g_ref, w_ref, kg_ref, o_ref, mn_s, *, k_tiles):
    @pl.when((pl.program_id(0) == 0) & (pl.program_id(1) == 0))
    def _():
        mn_s[...] = _rms_rows(mem_ref[...], g_ref[...]).astype(BF16)

    kv = _dot(mn_s[...], w_ref[0].astype(BF16))
    is_key = pl.program_id(1) < k_tiles

    @pl.when(is_key)
    def _():
        for c in range(kv.shape[1] // XA_DIM):
            kh = kv[:, c * XA_DIM:(c + 1) * XA_DIM]
            hs = jnp.mean(kh * kh, axis=-1, keepdims=True)
            o_ref[0, :, c * XA_DIM:(c + 1) * XA_DIM] = (
                (kh * lax.rsqrt(hs + EPS)) * kg_ref[0]).astype(o_ref.dtype)

    @pl.when(jnp.logical_not(is_key))
    def _():
        o_ref[0] = kv.astype(o_ref.dtype)


def _memkv(mem2, mem_g, w_kv, k_g, tn=512):
    depth, d, e = w_kv.shape
    mm = mem2.shape[0]
    return pl.pallas_call(
        functools.partial(_memkv_kernel, k_tiles=XA_WIDTH // tn),
        grid=(depth, e // tn),
        in_specs=[
            pl.BlockSpec((mm, d), lambda l, j: (0, 0)),
            pl.BlockSpec((1, d), lambda l, j: (0, 0)),
            pl.BlockSpec((1, d, tn), lambda l, j: (l, 0, j)),
            pl.BlockSpec((1, 1, XA_DIM), lambda l, j: (l, 0, 0)),
        ],
        out_specs=pl.BlockSpec((1, mm, tn), lambda l, j: (l, 0, j)),
        out_shape=jax.ShapeDtypeStruct((depth, mm, e), BF16),
        scratch_shapes=[pltpu.VMEM((mm, d), BF16)],
        compiler_params=_cparams(("arbitrary", "arbitrary")),
        name="mem_kv",
    )(mem2, mem_g.reshape(1, d), w_kv, k_g.reshape(depth, 1, XA_DIM))


def _xattn_kernel(q_ref, qg_ref, k_ref, v_ref, o_ref):
    for h in range(XA_HEADS):
        cs = slice(h * XA_DIM, (h + 1) * XA_DIM)
        q = q_ref[:, cs].astype(F32)
        ms = jnp.mean(q * q, axis=-1, keepdims=True)
        qn = ((q * lax.rsqrt(ms + EPS)) * qg_ref[...]).astype(BF16)
        s = _dot_nt(qn, k_ref[0, :, cs]) * (XA_DIM ** -0.5)
        e = jnp.exp(s - jnp.max(s, axis=-1, keepdims=True))
        p = e / jnp.sum(e, axis=-1, keepdims=True)
        o_ref[:, cs] = _dot(p.astype(BF16), v_ref[0, :, cs]).astype(o_ref.dtype)


def _xattn(proj, kv, q_g, layer, batch, xq_off, tm=512):
    m = proj.shape[0]
    n_mem = kv.shape[1] // batch
    nt = m // batch // tm
    qb = xq_off // XA_WIDTH
    return pl.pallas_call(
        _xattn_kernel,
        grid=(batch, nt),
        in_specs=[
            pl.BlockSpec((tm, XA_WIDTH), lambda b, i: (b * nt + i, qb)),
            pl.BlockSpec((1, XA_DIM), lambda b, i: (0, 0)),
            pl.BlockSpec((1, n_mem, XA_WIDTH), lambda b, i: (layer, b, 0)),
            pl.BlockSpec((1, n_mem, XA_WIDTH), lambda b, i: (layer, b, 1)),
        ],
        out_specs=pl.BlockSpec((tm, XA_WIDTH), lambda b, i: (b * nt + i, 0)),
        out_shape=jax.ShapeDtypeStruct((m, XA_WIDTH), BF16),
        compiler_params=_cparams(("parallel", "parallel")),
        name="mem_xattn",
    )(proj, q_g.reshape(1, XA_DIM), kv, kv)


def _outproj_kernel(*refs, emit_next):
    z_refs = refs[:4]
    mix_ref, xa_ref, x_ref, w_ref = refs[4:8]
    if emit_next:
        gn_ref, o_ref, hn_ref, y_s = refs[8:]
    else:
        o_ref, y_s = refs[8:]
    zw = z_refs[0].shape[1]
    n_mix = mix_ref.shape[1] // zw
    for c, z_ref in enumerate(z_refs):
        z = z_ref[...].astype(F32)
        if c < n_mix:
            br = mix_ref[:, c * zw:(c + 1) * zw]
        else:
            br = xa_ref[:, (c - n_mix) * zw:(c - n_mix + 1) * zw]
        y_s[:, c * zw:(c + 1) * zw] = (br.astype(F32) * (z * _sigmoid(z))).astype(BF16)
    out = x_ref[...] + _dot(y_s[...], w_ref[...])
    o_ref[...] = out
    if emit_next:
        hn_ref[...] = _rms_rows(out, gn_ref[...]).astype(BF16)


def _outproj(proj, z_off, mix, xa, x2, w, layer, next_g, tm=256):
    m, d = x2.shape
    inner = w.shape[1]
    mw = mix.shape[1]
    zw = inner // 4
    zb = z_off // zw
    emit_next = next_g is not None
    row_spec = pl.BlockSpec((tm, d), lambda i: (i, 0))
    in_specs = [pl.BlockSpec((tm, zw), functools.partial(lambda c, i: (i, zb + c), c))
                for c in range(4)]
    in_specs += [
        pl.BlockSpec((tm, mw), lambda i: (i, 0)),
        pl.BlockSpec((tm, inner - mw), lambda i: (i, 0)),
        row_spec,
        pl.BlockSpec((None, inner, d), lambda i: (layer, 0, 0), pipeline_mode=pl.Buffered(1)),
    ]
    operands = [proj, proj, proj, proj, mix, xa, x2, w]
    out_specs, out_shape = row_spec, jax.ShapeDtypeStruct((m, d), F32)
    if emit_next:
        in_specs.append(pl.BlockSpec((1, d), lambda i: (0, 0)))
        operands.append(next_g.reshape(1, d))
        out_specs = (row_spec, row_spec)
        out_shape = (out_shape, jax.ShapeDtypeStruct((m, d), BF16))
    return pl.pallas_call(
        functools.partial(_outproj_kernel, emit_next=emit_next),
        grid=(m // tm,),
        in_specs=in_specs,
        out_specs=out_specs,
        out_shape=out_shape,
        scratch_shapes=[pltpu.VMEM((tm, inner), BF16)],
        compiler_params=_cparams(("parallel",)),
        name="gate_outproj",
    )(*operands)


def _dn_gates_kernel(ab_ref, alog_ref, dtb_ref, o_ref, *, heads):
    s = ab_ref.shape[0]
    t = ab_ref[:, :LANES].astype(F32).T
    a = t[0:heads]
    b = t[heads:2 * heads]
    g = -jnp.exp(alog_ref[...]) * _softplus(a + dtb_ref[...])
    beta = _sigmoid(b)

    ii = _iota2((DN_PAIR, DN_PAIR), 0)
    jj = _iota2((DN_PAIR, DN_PAIR), 1)
    lo_i = ii < DN_CHUNK
    lo_j = jj < DN_CHUNK
    same = lo_i == lo_j
    one = lambda m: jnp.where(m, 1.0, 0.0).astype(BF16)
    rhs = jnp.concatenate(
        [one(same & (ii <= jj)), one(same), one(lo_i), one(jnp.logical_not(lo_i))], axis=1)

    for p in range(s // DN_PAIR):
        ls = slice(p * DN_PAIR, (p + 1) * DN_PAIR)
        hi, mid, lo = _split3(g[:, ls])
        r = _dot(hi, rhs) + _dot(mid, rhs) + _dot(lo, rhs)
        gc = r[:, 0:DN_PAIR]
        rows = (beta[:, ls], gc, r[:, DN_PAIR:2 * DN_PAIR] - gc,
                jnp.exp(r[:, 2 * DN_PAIR:3 * DN_PAIR]), jnp.exp(r[:, 3 * DN_PAIR:4 * DN_PAIR]))
        for n, val in enumerate(rows):
            o_ref[:, n, ls] = val
        for n in range(len(rows), N_GATE_ROWS):
            o_ref[:, n, ls] = jnp.zeros_like(gc)


def _dn_gates(proj, a_log, dt_bias, batch, ab_off):
    m = proj.shape[0]
    s = m // batch
    heads = a_log.shape[0]
    blk = 2 * LANES
    return pl.pallas_call(
        functools.partial(_dn_gates_kernel, heads=heads),
        grid=(batch,),
        in_specs=[
            pl.BlockSpec((s, blk), lambda b: (b, ab_off // blk)),
            pl.BlockSpec((heads, 1), lambda b: (0, 0)),
            pl.BlockSpec((heads, 1), lambda b: (0, 0)),
        ],
        out_specs=pl.BlockSpec((None, heads, N_GATE_ROWS, s), lambda b: (b, 0, 0, 0)),
        out_shape=jax.ShapeDtypeStruct((batch, heads, N_GATE_ROWS, s), F32),
        compiler_params=_cparams(("parallel",)),
        name="dn_gates",
    )(proj, a_log.reshape(heads, 1), dt_bias.reshape(heads, 1))


CONV_ROWS = 256


def _conv_silu_block(x_ref, w_ref, dst_ref, l2_scale, i):
    width = x_ref.shape[1]
    w = w_ref[...]
    row8 = _iota2((8, width), 0)
    r0 = pl.multiple_of(i * CONV_ROWS, CONV_ROWS)
    x = x_ref[pl.ds(r0, CONV_ROWS), :].astype(F32)
    p0 = pl.multiple_of(jnp.maximum(r0 - 16, 0), 16)
    prev = x_ref[pl.ds(p0, 16), :].astype(F32)[8:16]
    prev = prev * jnp.where(i > 0, 1.0, 0.0)
    acc = x * w[DN_CONV - 1:DN_CONV]
    for k in range(1, DN_CONV):
        xr = pltpu.roll(x, k, axis=0)
        pr = pltpu.roll(prev, k, axis=0)
        head = jnp.where(row8 < k, pr, xr[0:8])
        xr = jnp.concatenate([head, xr[8:]], axis=0)
        acc = acc + xr * w[DN_CONV - 1 - k:DN_CONV - k]
    y = acc * _sigmoid(acc)
    if l2_scale is not None:
        for c in range(width // HEAD_DIM):
            cs = slice(c * HEAD_DIM, (c + 1) * HEAD_DIM)
            yc = y[:, cs]
            ss = jnp.sum(yc * yc, axis=-1, keepdims=True)
            dst_ref[pl.ds(r0, CONV_ROWS), cs] = (yc * lax.rsqrt(ss + EPS)) * l2_scale
    else:
        dst_ref[pl.ds(r0, CONV_ROWS), :] = y


DN_QK_PER_STEP = 2
DN_LOCAL_PAIRS = 2


def _dn_kernel(q_ref, k_ref, v_ref, cwq_ref, cwk_ref, cwv_ref, gates_ref, og_ref, o_ref,
               qn_s, kn_s, vn_s, u0_s, wq_s, qkd_s, kdt_s, st_s):
    n_vh = 2 * DN_QK_PER_STEP
    head0 = n_vh * pl.program_id(1)
    s = q_ref.shape[0]
    n_pairs = s // DN_PAIR

    def conv_silu(x_ref, w_ref, dst_ref, l2_scale):
        def block(i, carry):
            _conv_silu_block(x_ref, w_ref, dst_ref, l2_scale, i)
            return carry

        lax.fori_loop(0, s // CONV_ROWS, block, 0)

    conv_silu(q_ref, cwq_ref, qn_s, HEAD_DIM ** -0.5)
    conv_silu(k_ref, cwk_ref, kn_s, 1.0)
    conv_silu(v_ref, cwv_ref, vn_s, None)

    ii = _iota2((DN_PAIR, DN_PAIR), 0)
    jj = _iota2((DN_PAIR, DN_PAIR), 1)
    eye = ii == jj
    incl = ((ii < DN_CHUNK) == (jj < DN_CHUNK)) & (ii >= jj)

    def gate_row(r, vh, r0):
        return gates_ref[head0 + vh, pl.ds(r, 1), pl.ds(r0, DN_PAIR)]

    def col(x):
        return jnp.broadcast_to(x, (DN_PAIR, DN_PAIR)).T

    def mm(a, b):
        return _dot(a.astype(BF16), b.astype(BF16))

    def hs(n):
        return slice(n * HEAD_DIM, (n + 1) * HEAD_DIM)

    def local_chain(p, qh, e, q, k, v, qk, kk):
        r0 = pl.multiple_of(p * DN_PAIR, DN_PAIR)
        vh = 2 * qh + e
        gc_r = gate_row(1, vh, r0)
        beta_c = col(gate_row(0, vh, r0))
        gc_c = col(gc_r)
        gl_c = col(gate_row(2, vh, r0))
        decay = jnp.exp(jnp.where(incl, gc_c - gc_r, NEG_BIG))
        a = jnp.where(eye, 0.0, beta_c * kk * decay)
        x = jnp.where(eye, 1.0, -a)
        pw = mm(a, a)
        yield
        for _ in range(4):
            x, pw = x + mm(x, pw), mm(pw, pw)
            yield
        x = x + mm(x, pw)
        ep_c = jnp.exp(gc_c)
        rhs = jnp.concatenate([v * beta_c, k * (beta_c * ep_c)], axis=1)
        yield
        sol = mm(x, rhs)
        qd = q * ep_c
        kdt = (k * jnp.exp(gl_c)).T
        qkd = qk * decay
        yield
        u0 = sol[:, :HEAD_DIM]
        w = sol[:, HEAD_DIM:]
        for cc in range(2):
            c = 2 * p + cc
            rs = slice(cc * DN_CHUNK, (cc + 1) * DN_CHUNK)
            u0_s[vh, c] = u0[rs]
            wq_s[vh, c] = jnp.concatenate([w[rs], qd[rs]], axis=0).astype(BF16)
            qkd_s[vh, c] = qkd[rs, rs].astype(BF16)
            kdt_s[vh, c] = kdt[:, rs].astype(BF16)

    def local(it, _):
        chains = []
        for pp in range(DN_LOCAL_PAIRS):
            p = DN_LOCAL_PAIRS * it + pp
            r0 = pl.multiple_of(p * DN_PAIR, DN_PAIR)
            for qh in range(DN_QK_PER_STEP):
                q = qn_s[pl.ds(r0, DN_PAIR), hs(qh)]
                k = kn_s[pl.ds(r0, DN_PAIR), hs(qh)]
                qk2 = _dot_nt(jnp.concatenate([q, k], axis=0).astype(BF16), k.astype(BF16))
                for e in range(2):
                    v = vn_s[pl.ds(r0, DN_PAIR), hs(2 * qh + e)]
                    chains.append(local_chain(p, qh, e, q, k, v, qk2[:DN_PAIR], qk2[DN_PAIR:]))
        _emit_staggered(chains, 0)
        return 0

    lax.fori_loop(0, n_pairs // DN_LOCAL_PAIRS, local, 0)

    og = og_ref[...]

    def recur_head(vh, p, cc):
        r0 = pl.multiple_of(p * DN_PAIR, DN_PAIR)
        c = 2 * p + cc
        row0 = pl.multiple_of(c * DN_CHUNK, DN_CHUNK)
        r = _dot(wq_s[vh, c], st_s[vh].astype(BF16))
        yield
        u = (u0_s[vh, c] - r[:DN_CHUNK]).astype(BF16)
        o = r[DN_CHUNK:] + _dot(qkd_s[vh, c], u)
        st_s[vh] = st_s[vh] * gate_row(3 + cc, vh, r0) + _dot(kdt_s[vh, c], u)
        yield
        ms = jnp.mean(o * o, axis=-1, keepdims=True)
        o_ref[pl.ds(row0, DN_CHUNK), hs(vh)] = ((o * lax.rsqrt(ms + EPS)) * og).astype(o_ref.dtype)

    def recur(p, _):
        for cc in range(2):
            _emit_staggered([recur_head(vh, p, cc) for vh in range(n_vh)], 0)
        return 0

    st_s[...] = jnp.zeros_like(st_s)
    lax.fori_loop(0, n_pairs, recur, 0)


def _deltanet(proj, conv_w, gates, out_g, batch, q_off, k_off, v_off):
    m = proj.shape[0]
    s = m // batch
    heads = gates.shape[1]
    n_vh = 2 * DN_QK_PER_STEP
    steps = heads // n_vh
    n_chunks = s // DN_CHUNK
    qw = DN_QK_PER_STEP * HEAD_DIM
    vw = n_vh * HEAD_DIM
    qb, kb, vb = q_off // qw, k_off // qw, v_off // vw
    ck = (heads // 2) * HEAD_DIM // qw
    cv = heads * HEAD_DIM // vw
    return pl.pallas_call(
        _dn_kernel,
        grid=(batch, steps),
        in_specs=[
            pl.BlockSpec((s, qw), lambda b, h: (b, qb + h)),
            pl.BlockSpec((s, qw), lambda b, h: (b, kb + h)),
            pl.BlockSpec((s, vw), lambda b, h: (b, vb + h)),
            pl.BlockSpec((DN_CONV, qw), lambda b, h: (0, h)),
            pl.BlockSpec((DN_CONV, qw), lambda b, h: (0, ck + h)),
            pl.BlockSpec((DN_CONV, vw), lambda b, h: (0, cv + h)),
            pl.BlockSpec((None, heads, N_GATE_ROWS, s), lambda b, h: (b, 0, 0, 0)),
            pl.BlockSpec((1, HEAD_DIM), lambda b, h: (0, 0)),
        ],
        out_specs=pl.BlockSpec((s, vw), lambda b, h: (b, h)),
        out_shape=jax.ShapeDtypeStruct((m, heads * HEAD_DIM), BF16),
        scratch_shapes=[
            pltpu.VMEM((s, qw), F32),
            pltpu.VMEM((s, qw), F32),
            pltpu.VMEM((s, vw), F32),
            pltpu.VMEM((n_vh, n_chunks, DN_CHUNK, HEAD_DIM), F32),
            pltpu.VMEM((n_vh, n_chunks, 2 * DN_CHUNK, HEAD_DIM), BF16),
            pltpu.VMEM((n_vh, n_chunks, DN_CHUNK, DN_CHUNK), BF16),
            pltpu.VMEM((n_vh, n_chunks, HEAD_DIM, DN_CHUNK), BF16),
            pltpu.VMEM((n_vh, HEAD_DIM, HEAD_DIM), F32),
        ],
        compiler_params=_cparams(("parallel", "arbitrary")),
        name="gated_deltanet",
    )(proj, proj, proj, conv_w, conv_w, conv_w, gates, out_g.reshape(1, HEAD_DIM))


SB_TILE = 128
SB_STATIC_SWEEPS = 3
SB_DONE = -110.0
SB_SKEW = 3


def _sb_kernel(q_ref, k_ref, v_ref, qg_ref, kg_ref, o_ref, qn_s, kn_s, acc_s, carry_s):
    s = q_ref.shape[0]
    t = SB_TILE
    nq = s // t

    def normalise(x_ref, g_ref, scale, dst):
        x = x_ref[...].astype(F32)
        ms = jnp.mean(x * x, axis=-1, keepdims=True)
        dst[...] = ((x * lax.rsqrt(ms + EPS)) * g_ref[...] * scale).astype(BF16)

    normalise(q_ref, qg_ref, HEAD_DIM ** -0.5, qn_s)
    normalise(k_ref, kg_ref, 1.0, kn_s)

    ii = _iota2((t, t), 0)
    jj = _iota2((t, t), 1)
    upper = jnp.where(ii > jj, 1.0, 0.0).astype(BF16)
    causal = jj < ii

    def tile(q0, k0, diagonal):
        z = _dot_nt(qn_s[pl.ds(q0, t), :], kn_s[pl.ds(k0, t), :])
        yield
        log_beta = jnp.minimum(z, 0.0) - jnp.log(1.0 + jnp.exp(-jnp.abs(z)))
        log_rest = log_beta - z
        if diagonal:
            log_rest = jnp.where(causal, log_rest, 0.0)
        hi = log_rest.astype(BF16)
        lo = (log_rest - hi.astype(F32)).astype(BF16)
        later = _dot(hi, upper) + _dot(lo, upper)
        row_sum = jnp.sum(log_rest, axis=-1, keepdims=True)
        yield
        if diagonal:
            w = jnp.where(causal, jnp.exp(log_beta + later), 0.0)
            acc_s[pl.ds(q0, t), :] = _dot(w.astype(BF16), v_ref[pl.ds(k0, t), :])
            carry_s[pl.ds(q0, t), :] = row_sum
        else:
            carry = carry_s[pl.ds(q0, t), :]
            w = jnp.exp(log_beta + (later + carry))
            acc_s[pl.ds(q0, t), :] += _dot(w.astype(BF16), v_ref[pl.ds(k0, t), :])
            carry_s[pl.ds(q0, t), :] = carry + row_sum

    always = [tile(i * t, (i - d) * t, d == 0)
              for d in range(SB_STATIC_SWEEPS) for i in range(d, nq)]
    _emit_staggered(always, 1, SB_SKEW)

    rows = _iota2((s, 1), 0)

    def live(d):
        return jnp.max(jnp.where(rows >= d * t, carry_s[...], SB_DONE)) > SB_DONE

    def sweep(d, _):
        @pl.when(live(d))
        def _():
            def body(i, _):
                for _ in tile(pl.multiple_of(i * t, t), pl.multiple_of((i - d) * t, t), False):
                    pass
                return 0

            lax.fori_loop(d, nq, body, 0)

        return 0

    @pl.when(live(SB_STATIC_SWEEPS))
    def _():
        lax.fori_loop(SB_STATIC_SWEEPS, nq, sweep, 0)

    o_ref[...] = acc_s[...].astype(o_ref.dtype)


def _stickbreaking(proj, q_g, k_g, batch, heads, q_off, k_off, v_off):
    m = proj.shape[0]
    s = m // batch
    qb, kb, vb = q_off // HEAD_DIM, k_off // HEAD_DIM, v_off // HEAD_DIM
    return pl.pallas_call(
        _sb_kernel,
        grid=(batch, heads),
        in_specs=[
            pl.BlockSpec((s, HEAD_DIM), lambda b, h: (b, qb + h)),
            pl.BlockSpec((s, HEAD_DIM), lambda b, h: (b, kb + h)),
            pl.BlockSpec((s, HEAD_DIM), lambda b, h: (b, vb + h)),
            pl.BlockSpec((1, HEAD_DIM), lambda b, h: (0, 0)),
            pl.BlockSpec((1, HEAD_DIM), lambda b, h: (0, 0)),
        ],
        out_specs=pl.BlockSpec((s, HEAD_DIM), lambda b, h: (b, h)),
        out_shape=jax.ShapeDtypeStruct((m, heads * HEAD_DIM), BF16),
        scratch_shapes=[
            pltpu.VMEM((s, HEAD_DIM), BF16),
            pltpu.VMEM((s, HEAD_DIM), BF16),
            pltpu.VMEM((s, HEAD_DIM), F32),
            pltpu.VMEM((s, 1), F32),
        ],
        compiler_params=_cparams(("parallel", "parallel")),
        name="stickbreaking_attn",
    )(proj, proj, proj, q_g.reshape(1, HEAD_DIM), k_g.reshape(1, HEAD_DIM))


def kernel(x, mem, norm_g, mem_norm_g, mem_w_kv, xa_q_norm_g, xa_k_norm_g, w_out,
           dn_w_in, dn_conv_w, dn_a_log, dn_dt_bias, dn_out_norm_g,
           sb_w_in, sb_q_norm_g, sb_k_norm_g):
    batch, seq, d = x.shape
    depth = norm_g.shape[0]
    inner = w_out.shape[1]
    mix_width = inner - XA_WIDTH
    heads = mix_width // HEAD_DIM
    m = batch * seq
    tn_dn = 1280

    dn_qk_width = (heads // 2) * HEAD_DIM
    dn_ab_off = 2 * dn_qk_width + mix_width
    dn_mix_cols = dn_ab_off + 2 * heads
    dn_mix_pad = -(-(dn_ab_off + 2 * LANES) // tn_dn) * tn_dn
    sb_mix_cols = 3 * mix_width

    kv = _memkv(mem.reshape(-1, d), mem_norm_g, mem_w_kv, xa_k_norm_g)
    w_out_b = w_out.astype(BF16)

    x2 = x.reshape(m, d)
    h = None
    for i in range(depth):
        j = i // 2
        src, g, normalise = (x2, norm_g[i], True) if h is None else (h, norm_g[i], False)
        if i % 2 == 0:
            w_t = jnp.swapaxes(dn_w_in[j], 0, 1)
            pmix = _inproj(src, g, w_t, dn_mix_pad, 1024, tn_dn, normalise, True)
            side = _inproj(src, g, w_t, w_t.shape[0] - dn_mix_cols, 1024, tn_dn, normalise, True,
                           col0=dn_mix_cols)
            xq_off = 0
            gates = _dn_gates(pmix, dn_a_log[j], dn_dt_bias[j], batch, dn_ab_off)
            mix = _deltanet(pmix, dn_conv_w[j], gates, dn_out_norm_g[j], batch,
                            0, dn_qk_width, 2 * dn_qk_width)
        else:
            side = _inproj(src, g, sb_w_in[j], sb_w_in.shape[2], 2048, 1024, normalise)
            xq_off = sb_mix_cols
            mix = _stickbreaking(side, sb_q_norm_g[j], sb_k_norm_g[j], batch, heads,
                                 0, mix_width, 2 * mix_width)
        xa = _xattn(side, kv, xa_q_norm_g[i], i, batch, xq_off)
        next_g = norm_g[i + 1] if i + 1 < depth else None
        res = _outproj(side, xq_off + XA_WIDTH, mix, xa, x2, w_out_b, i, next_g)
        x2, h = res if next_g is not None else (res, None)
    return x2.reshape(batch, seq, d)
```

```python
import functools

import jax
import jax.numpy as jnp
from jax import lax
from jax.experimental import pallas as pl
from jax.experimental.pallas import tpu as pltpu

F32 = jnp.float32
BF16 = jnp.bfloat16
EPS = 1e-6

XA_HEADS = 4
XA_DIM = 256
XA_WIDTH = XA_HEADS * XA_DIM
HEAD_DIM = 128
DN_CONV = 4
DN_CHUNK = 64
DN_PAIR = 2 * DN_CHUNK
N_GATE_ROWS = 8

LANES = 128
VMEM_BYTES_V7X = 64 * 1024 * 1024
VMEM_LIMIT = VMEM_BYTES_V7X - 8 * 1024 * 1024

NEG_BIG = -1e30


def _cparams(sem):
    return pltpu.CompilerParams(dimension_semantics=sem, vmem_limit_bytes=VMEM_LIMIT)


def _dot(a, b):
    return jnp.dot(a, b, preferred_element_type=F32)


def _dot_nt(a, b):
    return lax.dot_general(a, b, (((1,), (1,)), ((), ())), preferred_element_type=F32)


def _split3(x):
    hi = x.astype(BF16)
    r = x - hi.astype(F32)
    mid = r.astype(BF16)
    lo = (r - mid.astype(F32)).astype(BF16)
    return hi, mid, lo


def _sigmoid(x):
    return 1.0 / (1.0 + jnp.exp(-x))


def _softplus(x):
    return jnp.maximum(x, 0.0) + jnp.log(1.0 + jnp.exp(-jnp.abs(x)))


def _iota2(shape, dim):
    return lax.broadcasted_iota(jnp.int32, shape, dim)


def _emit_staggered(items, lag, skew=1):
    live = list(enumerate(items))
    stage = {n: 0 for n, _ in live}
    t = 0
    while live:
        due = [(n, g) for n, g in live if n * lag + stage[n] * skew <= t]
        for n, g in sorted(due, key=lambda ng: -stage[ng[0]]):
            try:
                next(g)
                stage[n] += 1
            except StopIteration:
                live.remove((n, g))
        t += 1


def _rms_rows(x, g):
    ms = jnp.mean(x * x, axis=-1, keepdims=True)
    return (x * lax.rsqrt(ms + EPS)) * g


def _inproj_kernel(x_ref, g_ref, w_ref, o_ref, wb_s, *, normalise, w_transposed):
    @pl.when(pl.program_id(1) == 0)
    def _():
        wb_s[...] = w_ref[...].astype(BF16)

    if normalise:
        h = _rms_rows(x_ref[...], g_ref[...]).astype(BF16)
    else:
        h = x_ref[...]
    acc = _dot_nt(h, wb_s[...]) if w_transposed else _dot(h, wb_s[...])
    o_ref[...] = acc.astype(o_ref.dtype)


def _inproj(x2, g, w, cols, tm, tn, normalise, w_transposed=False, col0=0):
    m, d = x2.shape
    if w_transposed:
        w_block = (tn, d)
        w_spec = pl.BlockSpec((pl.Element(tn), pl.Element(d)),
                              lambda j, i: (pl.multiple_of(col0 + j * tn, 8), 0))
    else:
        assert col0 == 0
        w_block, w_spec = (d, tn), pl.BlockSpec((d, tn), lambda j, i: (0, j))
    return pl.pallas_call(
        functools.partial(_inproj_kernel, normalise=normalise, w_transposed=w_transposed),
        grid=(cols // tn, m // tm),
        in_specs=[
            pl.BlockSpec((tm, d), lambda j, i: (i, 0)),
            pl.BlockSpec((1, d), lambda j, i: (0, 0)),
            w_spec,
        ],
        out_specs=pl.BlockSpec((tm, tn), lambda j, i: (i, j)),
        out_shape=jax.ShapeDtypeStruct((m, cols), BF16),
        scratch_shapes=[pltpu.VMEM(w_block, BF16)],
        compiler_params=_cparams(("arbitrary", "arbitrary")),
        name="rmsnorm_inproj",
    )(x2, g.reshape(1, d), w)


def _memkv_kernel(mem_ref, g_ref, w_ref, kg_ref, o_ref, mn_s, *, k_tiles):
    @pl.when((pl.program_id(0) == 0) & (pl.program_id(1) == 0))
    def _():
        mn_s[...] = _rms_rows(mem_ref[...], g_ref[...]).astype(BF16)

    kv = _dot(mn_s[...], w_ref[0].astype(BF16))
    is_key = pl.program_id(1) < k_tiles

    @pl.when(is_key)
    def _():
        for c in range(kv.shape[1] // XA_DIM):
            kh = kv[:, c * XA_DIM:(c + 1) * XA_DIM]
            hs = jnp.mean(kh * kh, axis=-1, keepdims=True)
            o_ref[0, :, c * XA_DIM:(c + 1) * XA_DIM] = (
                (kh * lax.rsqrt(hs + EPS)) * kg_ref[0]).astype(o_ref.dtype)

    @pl.when(jnp.logical_not(is_key))
    def _():
        o_ref[0] = kv.astype(o_ref.dtype)


def _memkv(mem2, mem_g, w_kv, k_g, tn=512):
    depth, d, e = w_kv.shape
    mm = mem2.shape[0]
    return pl.pallas_call(
        functools.partial(_memkv_kernel, k_tiles=XA_WIDTH // tn),
        grid=(depth, e // tn),
        in_specs=[
            pl.BlockSpec((mm, d), lambda l, j: (0, 0)),
            pl.BlockSpec((1, d), lambda l, j: (0, 0)),
            pl.BlockSpec((1, d, tn), lambda l, j: (l, 0, j)),
            pl.BlockSpec((1, 1, XA_DIM), lambda l, j: (l, 0, 0)),
        ],
        out_specs=pl.BlockSpec((1, mm, tn), lambda l, j: (l, 0, j)),
        out_shape=jax.ShapeDtypeStruct((depth, mm, e), BF16),
        scratch_shapes=[pltpu.VMEM((mm, d), BF16)],
        compiler_params=_cparams(("arbitrary", "arbitrary")),
        name="mem_kv",
    )(mem2, mem_g.reshape(1, d), w_kv, k_g.reshape(depth, 1, XA_DIM))


def _xattn_kernel(q_ref, qg_ref, k_ref, v_ref, o_ref):
    for h in range(XA_HEADS):
        cs = slice(h * XA_DIM, (h + 1) * XA_DIM)
        q = q_ref[:, cs].astype(F32)
        ms = jnp.mean(q * q, axis=-1, keepdims=True)
        qn = ((q * lax.rsqrt(ms + EPS)) * qg_ref[...]).astype(BF16)
        s = _dot_nt(qn, k_ref[0, :, cs]) * (XA_DIM ** -0.5)
        e = jnp.exp(s - jnp.max(s, axis=-1, keepdims=True))
        p = e / jnp.sum(e, axis=-1, keepdims=True)
        o_ref[:, cs] = _dot(p.astype(BF16), v_ref[0, :, cs]).astype(o_ref.dtype)


def _xattn(proj, kv, q_g, layer, batch, xq_off, tm=512):
    m = proj.shape[0]
    n_mem = kv.shape[1] // batch
    nt = m // batch // tm
    qb = xq_off // XA_WIDTH
    return pl.pallas_call(
        _xattn_kernel,
        grid=(batch, nt),
        in_specs=[
            pl.BlockSpec((tm, XA_WIDTH), lambda b, i: (b * nt + i, qb)),
            pl.BlockSpec((1, XA_DIM), lambda b, i: (0, 0)),
            pl.BlockSpec((1, n_mem, XA_WIDTH), lambda b, i: (layer, b, 0)),
            pl.BlockSpec((1, n_mem, XA_WIDTH), lambda b, i: (layer, b, 1)),
        ],
        out_specs=pl.BlockSpec((tm, XA_WIDTH), lambda b, i: (b * nt + i, 0)),
        out_shape=jax.ShapeDtypeStruct((m, XA_WIDTH), BF16),
        compiler_params=_cparams(("parallel", "parallel")),
        name="mem_xattn",
    )(proj, q_g.reshape(1, XA_DIM), kv, kv)


def _outproj_kernel(*refs, emit_next):
    z_refs = refs[:4]
    mix_ref, xa_ref, x_ref, w_ref = refs[4:8]
    if emit_next:
        gn_ref, o_ref, hn_ref, y_s = refs[8:]
    else:
        o_ref, y_s = refs[8:]
    zw = z_refs[0].shape[1]
    n_mix = mix_ref.shape[1] // zw
    for c, z_ref in enumerate(z_refs):
        z = z_ref[...].astype(F32)
        if c < n_mix:
            br = mix_ref[:, c * zw:(c + 1) * zw]
        else:
            br = xa_ref[:, (c - n_mix) * zw:(c - n_mix + 1) * zw]
        y_s[:, c * zw:(c + 1) * zw] = (br.astype(F32) * (z * _sigmoid(z))).astype(BF16)
    out = x_ref[...] + _dot(y_s[...], w_ref[...])
    o_ref[...] = out
    if emit_next:
        hn_ref[...] = _rms_rows(out, gn_ref[...]).astype(BF16)


def _outproj(proj, z_off, mix, xa, x2, w, layer, next_g, tm=256):
    m, d = x2.shape
    inner = w.shape[1]
    mw = mix.shape[1]
    zw = inner // 4
    zb = z_off // zw
    emit_next = next_g is not None
    row_spec = pl.BlockSpec((tm, d), lambda i: (i, 0))
    in_specs = [pl.BlockSpec((tm, zw), functools.partial(lambda c, i: (i, zb + c), c))
                for c in range(4)]
    in_specs += [
        pl.BlockSpec((tm, mw), lambda i: (i, 0)),
        pl.BlockSpec((tm, inner - mw), lambda i: (i, 0)),
        row_spec,
        pl.BlockSpec((None, inner, d), lambda i: (layer, 0, 0), pipeline_mode=pl.Buffered(1)),
    ]
    operands = [proj, proj, proj, proj, mix, xa, x2, w]
    out_specs, out_shape = row_spec, jax.ShapeDtypeStruct((m, d), F32)
    if emit_next:
        in_specs.append(pl.BlockSpec((1, d), lambda i: (0, 0)))
        operands.append(next_g.reshape(1, d))
        out_specs = (row_spec, row_spec)
        out_shape = (out_shape, jax.ShapeDtypeStruct((m, d), BF16))
    return pl.pallas_call(
        functools.partial(_outproj_kernel, emit_next=emit_next),
        grid=(m // tm,),
        in_specs=in_specs,
        out_specs=out_specs,
        out_shape=out_shape,
        scratch_shapes=[pltpu.VMEM((tm, inner), BF16)],
        compiler_params=_cparams(("parallel",)),
        name="gate_outproj",
    )(*operands)


def _dn_gates_kernel(ab_ref, alog_ref, dtb_ref, o_ref, *, heads):
    s = ab_ref.shape[0]
    t = ab_ref[:, :LANES].astype(F32).T
    a = t[0:heads]
    b = t[heads:2 * heads]
    g = -jnp.exp(alog_ref[...]) * _softplus(a + dtb_ref[...])
    beta = _sigmoid(b)

    ii = _iota2((DN_PAIR, DN_PAIR), 0)
    jj = _iota2((DN_PAIR, DN_PAIR), 1)
    lo_i = ii < DN_CHUNK
    lo_j = jj < DN_CHUNK
    same = lo_i == lo_j
    one = lambda m: jnp.where(m, 1.0, 0.0).astype(BF16)
    rhs = jnp.concatenate(
        [one(same & (ii <= jj)), one(same), one(lo_i), one(jnp.logical_not(lo_i))], axis=1)

    for p in range(s // DN_PAIR):
        ls = slice(p * DN_PAIR, (p + 1) * DN_PAIR)
        hi, mid, lo = _split3(g[:, ls])
        r = _dot(hi, rhs) + _dot(mid, rhs) + _dot(lo, rhs)
        gc = r[:, 0:DN_PAIR]
        rows = (beta[:, ls], gc, r[:, DN_PAIR:2 * DN_PAIR] - gc,
                jnp.exp(r[:, 2 * DN_PAIR:3 * DN_PAIR]), jnp.exp(r[:, 3 * DN_PAIR:4 * DN_PAIR]))
        for n, val in enumerate(rows):
            o_ref[:, n, ls] = val
        for n in range(len(rows), N_GATE_ROWS):
            o_ref[:, n, ls] = jnp.zeros_like(gc)


def _dn_gates(proj, a_log, dt_bias, batch, ab_off):
    m = proj.shape[0]
    s = m // batch
    heads = a_log.shape[0]
    blk = 2 * LANES
    return pl.pallas_call(
        functools.partial(_dn_gates_kernel, heads=heads),
        grid=(batch,),
        in_specs=[
            pl.BlockSpec((s, blk), lambda b: (b, ab_off // blk)),
            pl.BlockSpec((heads, 1), lambda b: (0, 0)),
            pl.BlockSpec((heads, 1), lambda b: (0, 0)),
        ],
        out_specs=pl.BlockSpec((None, heads, N_GATE_ROWS, s), lambda b: (b, 0, 0, 0)),
        out_shape=jax.ShapeDtypeStruct((batch, heads, N_GATE_ROWS, s), F32),
        compiler_params=_cparams(("parallel",)),
        name="dn_gates",
    )(proj, a_log.reshape(heads, 1), dt_bias.reshape(heads, 1))


CONV_ROWS = 256


def _conv_silu_block(x_ref, w_ref, dst_ref, l2_scale, i):
    width = x_ref.shape[1]
    w = w_ref[...]
    row8 = _iota2((8, width), 0)
    r0 = pl.multiple_of(i * CONV_ROWS, CONV_ROWS)
    x = x_ref[pl.ds(r0, CONV_ROWS), :].astype(F32)
    p0 = pl.multiple_of(jnp.maximum(r0 - 16, 0), 16)
    prev = x_ref[pl.ds(p0, 16), :].astype(F32)[8:16]
    prev = prev * jnp.where(i > 0, 1.0, 0.0)
    acc = x * w[DN_CONV - 1:DN_CONV]
    for k in range(1, DN_CONV):
        xr = pltpu.roll(x, k, axis=0)
        pr = pltpu.roll(prev, k, axis=0)
        head = jnp.where(row8 < k, pr, xr[0:8])
        xr = jnp.concatenate([head, xr[8:]], axis=0)
        acc = acc + xr * w[DN_CONV - 1 - k:DN_CONV - k]
    y = acc * _sigmoid(acc)
    if l2_scale is not None:
        for c in range(width // HEAD_DIM):
            cs = slice(c * HEAD_DIM, (c + 1) * HEAD_DIM)
            yc = y[:, cs]
            ss = jnp.sum(yc * yc, axis=-1, keepdims=True)
            dst_ref[pl.ds(r0, CONV_ROWS), cs] = (yc * lax.rsqrt(ss + EPS)) * l2_scale
    else:
        dst_ref[pl.ds(r0, CONV_ROWS), :] = y


DN_QK_PER_STEP = 2
DN_LOCAL_PAIRS = 2


def _dn_kernel(q_ref, k_ref, v_ref, cwq_ref, cwk_ref, cwv_ref, gates_ref, og_ref, o_ref,
               qn_s, kn_s, vn_s, u0_s, wq_s, qkd_s, kdt_s, st_s):
    n_vh = 2 * DN_QK_PER_STEP
    head0 = n_vh * pl.program_id(1)
    s = q_ref.shape[0]
    n_pairs = s // DN_PAIR

    def conv_silu(x_ref, w_ref, dst_ref, l2_scale):
        def block(i, carry):
            _conv_silu_block(x_ref, w_ref, dst_ref, l2_scale, i)
            return carry

        lax.fori_loop(0, s // CONV_ROWS, block, 0)

    conv_silu(q_ref, cwq_ref, qn_s, HEAD_DIM ** -0.5)
    conv_silu(k_ref, cwk_ref, kn_s, 1.0)
    conv_silu(v_ref, cwv_ref, vn_s, None)

    ii = _iota2((DN_PAIR, DN_PAIR), 0)
    jj = _iota2((DN_PAIR, DN_PAIR), 1)
    eye = ii == jj
    incl = ((ii < DN_CHUNK) == (jj < DN_CHUNK)) & (ii >= jj)

    def gate_row(r, vh, r0):
        return gates_ref[head0 + vh, pl.ds(r, 1), pl.ds(r0, DN_PAIR)]

    def col(x):
        return jnp.broadcast_to(x, (DN_PAIR, DN_PAIR)).T

    def mm(a, b):
        return _dot(a.astype(BF16), b.astype(BF16))

    def hs(n):
        return slice(n * HEAD_DIM, (n + 1) * HEAD_DIM)

    def local_chain(p, qh, e, q, k, v, qk, kk):
        r0 = pl.multiple_of(p * DN_PAIR, DN_PAIR)
        vh = 2 * qh + e
        gc_r = gate_row(1, vh, r0)
        beta_c = col(gate_row(0, vh, r0))
        gc_c = col(gc_r)
        gl_c = col(gate_row(2, vh, r0))
        decay = jnp.exp(jnp.where(incl, gc_c - gc_r, NEG_BIG))
        a = jnp.where(eye, 0.0, beta_c * kk * decay)
        x = jnp.where(eye, 1.0, -a)
        pw = mm(a, a)
        yield
        for _ in range(4):
            x, pw = x + mm(x, pw), mm(pw, pw)
            yield
        x = x + mm(x, pw)
        ep_c = jnp.exp(gc_c)
        rhs = jnp.concatenate([v * beta_c, k * (beta_c * ep_c)], axis=1)
        yield
        sol = mm(x, rhs)
        qd = q * ep_c
        kdt = (k * jnp.exp(gl_c)).T
        qkd = qk * decay
        yield
        u0 = sol[:, :HEAD_DIM]
        w = sol[:, HEAD_DIM:]
        for cc in range(2):
            c = 2 * p + cc
            rs = slice(cc * DN_CHUNK, (cc + 1) * DN_CHUNK)
            u0_s[vh, c] = u0[rs]
            wq_s[vh, c] = jnp.concatenate([w[rs], qd[rs]], axis=0).astype(BF16)
            qkd_s[vh, c] = qkd[rs, rs].astype(BF16)
            kdt_s[vh, c] = kdt[:, rs].astype(BF16)

    def local(it, _):
        chains = []
        for pp in range(DN_LOCAL_PAIRS):
            p = DN_LOCAL_PAIRS * it + pp
            r0 = pl.multiple_of(p * DN_PAIR, DN_PAIR)
            for qh in range(DN_QK_PER_STEP):
                q = qn_s[pl.ds(r0, DN_PAIR), hs(qh)]
                k = kn_s[pl.ds(r0, DN_PAIR), hs(qh)]
                qk2 = _dot_nt(jnp.concatenate([q, k], axis=0).astype(BF16), k.astype(BF16))
                for e in range(2):
                    v = vn_s[pl.ds(r0, DN_PAIR), hs(2 * qh + e)]
                    chains.append(local_chain(p, qh, e, q, k, v, qk2[:DN_PAIR], qk2[DN_PAIR:]))
        _emit_staggered(chains, 0)
        return 0

    lax.fori_loop(0, n_pairs // DN_LOCAL_PAIRS, local, 0)

    og = og_ref[...]

    def recur_head(vh, p, cc):
        r0 = pl.multiple_of(p * DN_PAIR, DN_PAIR)
        c = 2 * p + cc
        row0 = pl.multiple_of(c * DN_CHUNK, DN_CHUNK)
        r = _dot(wq_s[vh, c], st_s[vh].astype(BF16))
        yield
        u = (u0_s[vh, c] - r[:DN_CHUNK]).astype(BF16)
        o = r[DN_CHUNK:] + _dot(qkd_s[vh, c], u)
        st_s[vh] = st_s[vh] * gate_row(3 + cc, vh, r0) + _dot(kdt_s[vh, c], u)
        yield
        ms = jnp.mean(o * o, axis=-1, keepdims=True)
        o_ref[pl.ds(row0, DN_CHUNK), hs(vh)] = ((o * lax.rsqrt(ms + EPS)) * og).astype(o_ref.dtype)

    def recur(p, _):
        for cc in range(2):
            _emit_staggered([recur_head(vh, p, cc) for vh in range(n_vh)], 0)
        return 0

    st_s[...] = jnp.zeros_like(st_s)
    lax.fori_loop(0, n_pairs, recur, 0)


def _deltanet(proj, conv_w, gates, out_g, batch, q_off, k_off, v_off):
    m = proj.shape[0]
    s = m // batch
    heads = gates.shape[1]
    n_vh = 2 * DN_QK_PER_STEP
    steps = heads // n_vh
    n_chunks = s // DN_CHUNK
    qw = DN_QK_PER_STEP * HEAD_DIM
    vw = n_vh * HEAD_DIM
    qb, kb, vb = q_off // qw, k_off // qw, v_off // vw
    ck = (heads // 2) * HEAD_DIM // qw
    cv = heads * HEAD_DIM // vw
    return pl.pallas_call(
        _dn_kernel,
        grid=(batch, steps),
        in_specs=[
            pl.BlockSpec((s, qw), lambda b, h: (b, qb + h)),
            pl.BlockSpec((s, qw), lambda b, h: (b, kb + h)),
            pl.BlockSpec((s, vw), lambda b, h: (b, vb + h)),
            pl.BlockSpec((DN_CONV, qw), lambda b, h: (0, h)),
            pl.BlockSpec((DN_CONV, qw), lambda b, h: (0, ck + h)),
            pl.BlockSpec((DN_CONV, vw), lambda b, h: (0, cv + h)),
            pl.BlockSpec((None, heads, N_GATE_ROWS, s), lambda b, h: (b, 0, 0, 0)),
            pl.BlockSpec((1, HEAD_DIM), lambda b, h: (0, 0)),
        ],
        out_specs=pl.BlockSpec((s, vw), lambda b, h: (b, h)),
        out_shape=jax.ShapeDtypeStruct((m, heads * HEAD_DIM), BF16),
        scratch_shapes=[
            pltpu.VMEM((s, qw), F32),
            pltpu.VMEM((s, qw), F32),
            pltpu.VMEM((s, vw), F32),
            pltpu.VMEM((n_vh, n_chunks, DN_CHUNK, HEAD_DIM), F32),
            pltpu.VMEM((n_vh, n_chunks, 2 * DN_CHUNK, HEAD_DIM), BF16),
            pltpu.VMEM((n_vh, n_chunks, DN_CHUNK, DN_CHUNK), BF16),
            pltpu.VMEM((n_vh, n_chunks, HEAD_DIM, DN_CHUNK), BF16),
            pltpu.VMEM((n_vh, HEAD_DIM, HEAD_DIM), F32),
        ],
        compiler_params=_cparams(("parallel", "arbitrary")),
        name="gated_deltanet",
    )(proj, proj, proj, conv_w, conv_w, conv_w, gates, out_g.reshape(1, HEAD_DIM))


SB_TILE = 128
SB_STATIC_SWEEPS = 3
SB_DONE = -110.0
SB_SKEW = 3


SB_HEADS_PER_STEP = 2


def _sb_kernel(q_ref, k_ref, v_ref, qg_ref, kg_ref, o_ref, qn_s, kn_s, acc_s, carry_s):
    s = q_ref.shape[0]
    t = SB_TILE
    nq = s // t
    n_heads = SB_HEADS_PER_STEP

    def hs(h):
        return slice(h * HEAD_DIM, (h + 1) * HEAD_DIM)

    def normalise(x_ref, g_ref, scale, dst):
        for h in range(n_heads):
            x = x_ref[:, hs(h)].astype(F32)
            ms = jnp.mean(x * x, axis=-1, keepdims=True)
            dst[:, hs(h)] = ((x * lax.rsqrt(ms + EPS)) * g_ref[...] * scale).astype(BF16)

    normalise(q_ref, qg_ref, HEAD_DIM ** -0.5, qn_s)
    normalise(k_ref, kg_ref, 1.0, kn_s)

    ii = _iota2((t, t), 0)
    jj = _iota2((t, t), 1)
    upper = jnp.where(ii > jj, 1.0, 0.0).astype(BF16)
    causal = jj < ii

    def tile(h, q0, k0, diagonal):
        z = _dot_nt(qn_s[pl.ds(q0, t), hs(h)], kn_s[pl.ds(k0, t), hs(h)])
        yield
        log_beta = jnp.minimum(z, 0.0) - jnp.log(1.0 + jnp.exp(-jnp.abs(z)))
        log_rest = log_beta - z
        if diagonal:
            log_rest = jnp.where(causal, log_rest, 0.0)
        hi = log_rest.astype(BF16)
        lo = (log_rest - hi.astype(F32)).astype(BF16)
        later = _dot(hi, upper) + _dot(lo, upper)
        row_sum = jnp.sum(log_rest, axis=-1, keepdims=True)
        yield
        if diagonal:
            w = jnp.where(causal, jnp.exp(log_beta + later), 0.0)
            acc_s[pl.ds(q0, t), hs(h)] = _dot(w.astype(BF16), v_ref[pl.ds(k0, t), hs(h)])
            carry_s[h, pl.ds(q0, t), :] = row_sum
        else:
            carry = carry_s[h, pl.ds(q0, t), :]
            w = jnp.exp(log_beta + (later + carry))
            acc_s[pl.ds(q0, t), hs(h)] += _dot(w.astype(BF16), v_ref[pl.ds(k0, t), hs(h)])
            carry_s[h, pl.ds(q0, t), :] = carry + row_sum

    always = [tile(h, i * t, (i - d) * t, d == 0)
              for d in range(SB_STATIC_SWEEPS) for i in range(d, nq) for h in range(n_heads)]
    _emit_staggered(always, 1, SB_SKEW)

    rows = _iota2((s, 1), 0)

    def live(d):
        worst = jnp.maximum(*[jnp.max(jnp.where(rows >= d * t, carry_s[h], SB_DONE))
                              for h in range(n_heads)])
        return worst > SB_DONE

    def sweep(d, _):
        @pl.when(live(d))
        def _():
            def body(i, _):
                for h in range(n_heads):
                    for _ in tile(h, pl.multiple_of(i * t, t), pl.multiple_of((i - d) * t, t),
                                  False):
                        pass
                return 0

            lax.fori_loop(d, nq, body, 0)

        return 0

    @pl.when(live(SB_STATIC_SWEEPS))
    def _():
        lax.fori_loop(SB_STATIC_SWEEPS, nq, sweep, 0)

    o_ref[...] = acc_s[...].astype(o_ref.dtype)


def _stickbreaking(proj, q_g, k_g, batch, heads, q_off, k_off, v_off):
    m = proj.shape[0]
    s = m // batch
    hw = SB_HEADS_PER_STEP * HEAD_DIM
    qb, kb, vb = q_off // hw, k_off // hw, v_off // hw
    return pl.pallas_call(
        _sb_kernel,
        grid=(batch, heads // SB_HEADS_PER_STEP),
        in_specs=[
            pl.BlockSpec((s, hw), lambda b, h: (b, qb + h)),
            pl.BlockSpec((s, hw), lambda b, h: (b, kb + h)),
            pl.BlockSpec((s, hw), lambda b, h: (b, vb + h)),
            pl.BlockSpec((1, HEAD_DIM), lambda b, h: (0, 0)),
            pl.BlockSpec((1, HEAD_DIM), lambda b, h: (0, 0)),
        ],
        out_specs=pl.BlockSpec((s, hw), lambda b, h: (b, h)),
        out_shape=jax.ShapeDtypeStruct((m, heads * HEAD_DIM), BF16),
        scratch_shapes=[
            pltpu.VMEM((s, hw), BF16),
            pltpu.VMEM((s, hw), BF16),
            pltpu.VMEM((s, hw), F32),
            pltpu.VMEM((SB_HEADS_PER_STEP, s, 1), F32),
        ],
        compiler_params=_cparams(("parallel", "parallel")),
        name="stickbreaking_attn",
    )(proj, proj, proj, q_g.reshape(1, HEAD_DIM), k_g.reshape(1, HEAD_DIM))


def kernel(x, mem, norm_g, mem_norm_g, mem_w_kv, xa_q_norm_g, xa_k_norm_g, w_out,
           dn_w_in, dn_conv_w, dn_a_log, dn_dt_bias, dn_out_norm_g,
           sb_w_in, sb_q_norm_g, sb_k_norm_g):
    batch, seq, d = x.shape
    depth = norm_g.shape[0]
    inner = w_out.shape[1]
    mix_width = inner - XA_WIDTH
    heads = mix_width // HEAD_DIM
    m = batch * seq
    tn_dn = 1280

    dn_qk_width = (heads // 2) * HEAD_DIM
    dn_ab_off = 2 * dn_qk_width + mix_width
    dn_mix_cols = dn_ab_off + 2 * heads
    dn_mix_pad = -(-(dn_ab_off + 2 * LANES) // tn_dn) * tn_dn
    sb_mix_cols = 3 * mix_width

    kv = _memkv(mem.reshape(-1, d), mem_norm_g, mem_w_kv, xa_k_norm_g)
    w_out_b = w_out.astype(BF16)

    x2 = x.reshape(m, d)
    h = None
    for i in range(depth):
        j = i // 2
        src, g, normalise = (x2, norm_g[i], True) if h is None else (h, norm_g[i], False)
        if i % 2 == 0:
            w_t = jnp.swapaxes(dn_w_in[j], 0, 1)
            pmix = _inproj(src, g, w_t, dn_mix_pad, 1024, tn_dn, normalise, True)
            side = _inproj(src, g, w_t, w_t.shape[0] - dn_mix_cols, 1024, tn_dn, normalise, True,
                           col0=dn_mix_cols)
            xq_off = 0
            gates = _dn_gates(pmix, dn_a_log[j], dn_dt_bias[j], batch, dn_ab_off)
            mix = _deltanet(pmix, dn_conv_w[j], gates, dn_out_norm_g[j], batch,
                            0, dn_qk_width, 2 * dn_qk_width)
        else:
            side = _inproj(src, g, sb_w_in[j], sb_w_in.shape[2], 2048, 1024, normalise)
            xq_off = sb_mix_cols
            mix = _stickbreaking(side, sb_q_norm_g[j], sb_k_norm_g[j], batch, heads,
                                 0, mix_width, 2 * mix_width)
        xa = _xattn(side, kv, xa_q_norm_g[i], i, batch, xq_off)
        next_g = norm_g[i + 1] if i + 1 < depth else None
        res = _outproj(side, xq_off + XA_WIDTH, mix, xa, x2, w_out_b, i, next_g)
        x2, h = res if next_g is not None else (res, None)
    return x2.reshape(batch, seq, d)
```

```python
import functools

import jax
import jax.numpy as jnp
from jax import lax
from jax.experimental import pallas as pl
from jax.experimental.pallas import tpu as pltpu

F32 = jnp.float32
BF16 = jnp.bfloat16
EPS = 1e-6

XA_HEADS = 4
XA_DIM = 256
XA_WIDTH = XA_HEADS * XA_DIM
HEAD_DIM = 128
DN_CONV = 4
DN_CHUNK = 64
DN_PAIR = 2 * DN_CHUNK
N_GATE_ROWS = 8

LANES = 128
VMEM_BYTES_V7X = 64 * 1024 * 1024
VMEM_LIMIT = VMEM_BYTES_V7X - 8 * 1024 * 1024

NEG_BIG = -1e30


def _cparams(sem):
    return pltpu.CompilerParams(dimension_semantics=sem, vmem_limit_bytes=VMEM_LIMIT)


def _dot(a, b):
    return jnp.dot(a, b, preferred_element_type=F32)


def _dot_nt(a, b):
    return lax.dot_general(a, b, (((1,), (1,)), ((), ())), preferred_element_type=F32)


def _split3(x):
    hi = x.astype(BF16)
    r = x - hi.astype(F32)
    mid = r.astype(BF16)
    lo = (r - mid.astype(F32)).astype(BF16)
    return hi, mid, lo


def _sigmoid(x):
    return 1.0 / (1.0 + jnp.exp(-x))


def _softplus(x):
    return jnp.maximum(x, 0.0) + jnp.log(1.0 + jnp.exp(-jnp.abs(x)))


def _iota2(shape, dim):
    return lax.broadcasted_iota(jnp.int32, shape, dim)


def _emit_staggered(items, lag, skew=1):
    live = list(enumerate(items))
    stage = {n: 0 for n, _ in live}
    t = 0
    while live:
        due = [(n, g) for n, g in live if n * lag + stage[n] * skew <= t]
        for n, g in sorted(due, key=lambda ng: -stage[ng[0]]):
            try:
                next(g)
                stage[n] += 1
            except StopIteration:
                live.remove((n, g))
        t += 1


def _rms_rows(x, g):
    ms = jnp.mean(x * x, axis=-1, keepdims=True)
    return (x * lax.rsqrt(ms + EPS)) * g


def _inproj_kernel(*refs, normalise, w_transposed, with_cast):
    if with_cast:
        x_ref, g_ref, w_ref, cast_ref, o_ref, cast_o_ref, wb_s = refs
        cast_o_ref[...] = cast_ref[...].astype(BF16)
    else:
        x_ref, g_ref, w_ref, o_ref, wb_s = refs

    @pl.when(pl.program_id(1) == 0)
    def _():
        wb_s[...] = w_ref[...].astype(BF16)

    if normalise:
        h = _rms_rows(x_ref[...], g_ref[...]).astype(BF16)
    else:
        h = x_ref[...]
    acc = _dot_nt(h, wb_s[...]) if w_transposed else _dot(h, wb_s[...])
    o_ref[...] = acc.astype(o_ref.dtype)


def _inproj(x2, g, w, cols, tm, tn, normalise, w_transposed=False, col0=0, cast_rows=None):
    m, d = x2.shape
    if w_transposed:
        w_block = (tn, d)
        w_spec = pl.BlockSpec((pl.Element(tn), pl.Element(d)),
                              lambda j, i: (pl.multiple_of(col0 + j * tn, 8), 0))
    else:
        assert col0 == 0
        w_block, w_spec = (d, tn), pl.BlockSpec((d, tn), lambda j, i: (0, j))
    n_j, n_i = cols // tn, m // tm
    in_specs = [
        pl.BlockSpec((tm, d), lambda j, i: (i, 0)),
        pl.BlockSpec((1, d), lambda j, i: (0, 0)),
        w_spec,
    ]
    operands = [x2, g.reshape(1, d), w]
    out_specs = pl.BlockSpec((tm, tn), lambda j, i: (i, j))
    out_shape = jax.ShapeDtypeStruct((m, cols), BF16)
    if cast_rows is not None:
        r, c = cast_rows.shape
        slab = r // (n_j * n_i)
        assert slab * n_j * n_i == r
        slab_spec = pl.BlockSpec((slab, c), lambda j, i: (j * n_i + i, 0))
        in_specs.append(slab_spec)
        operands.append(cast_rows)
        out_specs = (out_specs, slab_spec)
        out_shape = (out_shape, jax.ShapeDtypeStruct((r, c), BF16))
    return pl.pallas_call(
        functools.partial(_inproj_kernel, normalise=normalise, w_transposed=w_transposed,
                          with_cast=cast_rows is not None),
        grid=(n_j, n_i),
        in_specs=in_specs,
        out_specs=out_specs,
        out_shape=out_shape,
        scratch_shapes=[pltpu.VMEM(w_block, BF16)],
        compiler_params=_cparams(("arbitrary", "arbitrary")),
        name="rmsnorm_inproj",
    )(*operands)


def _memkv_kernel(mem_ref, g_ref, w_ref, kg_ref, o_ref, mn_s, *, k_tiles):
    @pl.when((pl.program_id(0) == 0) & (pl.program_id(1) == 0))
    def _():
        mn_s[...] = _rms_rows(mem_ref[...], g_ref[...]).astype(BF16)

    kv = _dot(mn_s[...], w_ref[0].astype(BF16))
    is_key = pl.program_id(1) < k_tiles

    @pl.when(is_key)
    def _():
        for c in range(kv.shape[1] // XA_DIM):
            kh = kv[:, c * XA_DIM:(c + 1) * XA_DIM]
            hs = jnp.mean(kh * kh, axis=-1, keepdims=True)
            o_ref[0, :, c * XA_DIM:(c + 1) * XA_DIM] = (
                (kh * lax.rsqrt(hs + EPS)) * kg_ref[0]).astype(o_ref.dtype)

    @pl.when(jnp.logical_not(is_key))
    def _():
        o_ref[0] = kv.astype(o_ref.dtype)


def _memkv(mem2, mem_g, w_kv, k_g, tn=512):
    depth, d, e = w_kv.shape
    mm = mem2.shape[0]
    return pl.pallas_call(
        functools.partial(_memkv_kernel, k_tiles=XA_WIDTH // tn),
        grid=(depth, e // tn),
        in_specs=[
            pl.BlockSpec((mm, d), lambda l, j: (0, 0)),
            pl.BlockSpec((1, d), lambda l, j: (0, 0)),
            pl.BlockSpec((1, d, tn), lambda l, j: (l, 0, j)),
            pl.BlockSpec((1, 1, XA_DIM), lambda l, j: (l, 0, 0)),
        ],
        out_specs=pl.BlockSpec((1, mm, tn), lambda l, j: (l, 0, j)),
        out_shape=jax.ShapeDtypeStruct((depth, mm, e), BF16),
        scratch_shapes=[pltpu.VMEM((mm, d), BF16)],
        compiler_params=_cparams(("arbitrary", "arbitrary")),
        name="mem_kv",
    )(mem2, mem_g.reshape(1, d), w_kv, k_g.reshape(depth, 1, XA_DIM))


def _xattn_kernel(q_ref, qg_ref, k_ref, v_ref, o_ref):
    for h in range(XA_HEADS):
        cs = slice(h * XA_DIM, (h + 1) * XA_DIM)
        q = q_ref[:, cs].astype(F32)
        ms = jnp.mean(q * q, axis=-1, keepdims=True)
        qn = ((q * lax.rsqrt(ms + EPS)) * qg_ref[...]).astype(BF16)
        s = _dot_nt(qn, k_ref[0, :, cs]) * (XA_DIM ** -0.5)
        e = jnp.exp(s - jnp.max(s, axis=-1, keepdims=True))
        p = e / jnp.sum(e, axis=-1, keepdims=True)
        o_ref[:, cs] = _dot(p.astype(BF16), v_ref[0, :, cs]).astype(o_ref.dtype)


def _xattn(proj, kv, q_g, layer, batch, xq_off, tm=512):
    m = proj.shape[0]
    n_mem = kv.shape[1] // batch
    nt = m // batch // tm
    qb = xq_off // XA_WIDTH
    return pl.pallas_call(
        _xattn_kernel,
        grid=(batch, nt),
        in_specs=[
            pl.BlockSpec((tm, XA_WIDTH), lambda b, i: (b * nt + i, qb)),
            pl.BlockSpec((1, XA_DIM), lambda b, i: (0, 0)),
            pl.BlockSpec((1, n_mem, XA_WIDTH), lambda b, i: (layer, b, 0)),
            pl.BlockSpec((1, n_mem, XA_WIDTH), lambda b, i: (layer, b, 1)),
        ],
        out_specs=pl.BlockSpec((tm, XA_WIDTH), lambda b, i: (b * nt + i, 0)),
        out_shape=jax.ShapeDtypeStruct((m, XA_WIDTH), BF16),
        compiler_params=_cparams(("parallel", "parallel")),
        name="mem_xattn",
    )(proj, q_g.reshape(1, XA_DIM), kv, kv)


def _outproj_kernel(*refs, emit_next):
    z_refs = refs[:4]
    mix_ref, xa_ref, x_ref, w_ref = refs[4:8]
    if emit_next:
        gn_ref, o_ref, hn_ref, y_s = refs[8:]
    else:
        o_ref, y_s = refs[8:]
    zw = z_refs[0].shape[1]
    n_mix = mix_ref.shape[1] // zw
    for c, z_ref in enumerate(z_refs):
        z = z_ref[...].astype(F32)
        if c < n_mix:
            br = mix_ref[:, c * zw:(c + 1) * zw]
        else:
            br = xa_ref[:, (c - n_mix) * zw:(c - n_mix + 1) * zw]
        y_s[:, c * zw:(c + 1) * zw] = (br.astype(F32) * (z * _sigmoid(z))).astype(BF16)
    out = x_ref[...] + _dot(y_s[...], w_ref[...])
    o_ref[...] = out
    if emit_next:
        hn_ref[...] = _rms_rows(out, gn_ref[...]).astype(BF16)


def _outproj(proj, z_off, mix, xa, x2, w, layer, next_g, tm=256):
    m, d = x2.shape
    inner = w.shape[1]
    mw = mix.shape[1]
    zw = inner // 4
    zb = z_off // zw
    emit_next = next_g is not None
    row_spec = pl.BlockSpec((tm, d), lambda i: (i, 0))
    in_specs = [pl.BlockSpec((tm, zw), functools.partial(lambda c, i: (i, zb + c), c))
                for c in range(4)]
    in_specs += [
        pl.BlockSpec((tm, mw), lambda i: (i, 0)),
        pl.BlockSpec((tm, inner - mw), lambda i: (i, 0)),
        row_spec,
        pl.BlockSpec((None, inner, d), lambda i: (layer, 0, 0), pipeline_mode=pl.Buffered(1)),
    ]
    operands = [proj, proj, proj, proj, mix, xa, x2, w]
    out_specs, out_shape = row_spec, jax.ShapeDtypeStruct((m, d), F32)
    if emit_next:
        in_specs.append(pl.BlockSpec((1, d), lambda i: (0, 0)))
        operands.append(next_g.reshape(1, d))
        out_specs = (row_spec, row_spec)
        out_shape = (out_shape, jax.ShapeDtypeStruct((m, d), BF16))
    return pl.pallas_call(
        functools.partial(_outproj_kernel, emit_next=emit_next),
        grid=(m // tm,),
        in_specs=in_specs,
        out_specs=out_specs,
        out_shape=out_shape,
        scratch_shapes=[pltpu.VMEM((tm, inner), BF16)],
        compiler_params=_cparams(("parallel",)),
        name="gate_outproj",
    )(*operands)


def _dn_gates_kernel(ab_ref, alog_ref, dtb_ref, o_ref, *, heads):
    s = ab_ref.shape[0]
    t = ab_ref[:, :LANES].astype(F32).T
    a = t[0:heads]
    b = t[heads:2 * heads]
    g = -jnp.exp(alog_ref[...]) * _softplus(a + dtb_ref[...])
    beta = _sigmoid(b)

    ii = _iota2((DN_PAIR, DN_PAIR), 0)
    jj = _iota2((DN_PAIR, DN_PAIR), 1)
    lo_i = ii < DN_CHUNK
    lo_j = jj < DN_CHUNK
    same = lo_i == lo_j
    one = lambda m: jnp.where(m, 1.0, 0.0).astype(BF16)
    rhs = jnp.concatenate(
        [one(same & (ii <= jj)), one(same), one(lo_i), one(jnp.logical_not(lo_i))], axis=1)

    for p in range(s // DN_PAIR):
        ls = slice(p * DN_PAIR, (p + 1) * DN_PAIR)
        hi, mid, lo = _split3(g[:, ls])
        r = _dot(hi, rhs) + _dot(mid, rhs) + _dot(lo, rhs)
        gc = r[:, 0:DN_PAIR]
        rows = (beta[:, ls], gc, r[:, DN_PAIR:2 * DN_PAIR] - gc,
                jnp.exp(r[:, 2 * DN_PAIR:3 * DN_PAIR]), jnp.exp(r[:, 3 * DN_PAIR:4 * DN_PAIR]))
        for n, val in enumerate(rows):
            o_ref[:, n, ls] = val
        for n in range(len(rows), N_GATE_ROWS):
            o_ref[:, n, ls] = jnp.zeros_like(gc)


def _dn_gates(proj, a_log, dt_bias, batch, ab_off):
    m = proj.shape[0]
    s = m // batch
    heads = a_log.shape[0]
    blk = 2 * LANES
    return pl.pallas_call(
        functools.partial(_dn_gates_kernel, heads=heads),
        grid=(batch,),
        in_specs=[
            pl.BlockSpec((s, blk), lambda b: (b, ab_off // blk)),
            pl.BlockSpec((heads, 1), lambda b: (0, 0)),
            pl.BlockSpec((heads, 1), lambda b: (0, 0)),
        ],
        out_specs=pl.BlockSpec((None, heads, N_GATE_ROWS, s), lambda b: (b, 0, 0, 0)),
        out_shape=jax.ShapeDtypeStruct((batch, heads, N_GATE_ROWS, s), F32),
        compiler_params=_cparams(("parallel",)),
        name="dn_gates",
    )(proj, a_log.reshape(heads, 1), dt_bias.reshape(heads, 1))


CONV_ROWS = 256


def _conv_silu_block(x_ref, w_ref, dst_ref, l2_scale, i):
    width = x_ref.shape[1]
    w = w_ref[...]
    row8 = _iota2((8, width), 0)
    r0 = pl.multiple_of(i * CONV_ROWS, CONV_ROWS)
    x = x_ref[pl.ds(r0, CONV_ROWS), :].astype(F32)
    p0 = pl.multiple_of(jnp.maximum(r0 - 16, 0), 16)
    prev = x_ref[pl.ds(p0, 16), :].astype(F32)[8:16]
    prev = prev * jnp.where(i > 0, 1.0, 0.0)
    acc = x * w[DN_CONV - 1:DN_CONV]
    for k in range(1, DN_CONV):
        xr = pltpu.roll(x, k, axis=0)
        pr = pltpu.roll(prev, k, axis=0)
        head = jnp.where(row8 < k, pr, xr[0:8])
        xr = jnp.concatenate([head, xr[8:]], axis=0)
        acc = acc + xr * w[DN_CONV - 1 - k:DN_CONV - k]
    y = acc * _sigmoid(acc)
    if l2_scale is not None:
        for c in range(width // HEAD_DIM):
            cs = slice(c * HEAD_DIM, (c + 1) * HEAD_DIM)
            yc = y[:, cs]
            ss = jnp.sum(yc * yc, axis=-1, keepdims=True)
            dst_ref[pl.ds(r0, CONV_ROWS), cs] = (yc * lax.rsqrt(ss + EPS)) * l2_scale
    else:
        dst_ref[pl.ds(r0, CONV_ROWS), :] = y


DN_QK_PER_STEP = 2
DN_LOCAL_PAIRS = 2


def _dn_kernel(q_ref, k_ref, v_ref, cwq_ref, cwk_ref, cwv_ref, gates_ref, og_ref, o_ref,
               qn_s, kn_s, vn_s, u0_s, wq_s, qkd_s, kdt_s, st_s):
    n_vh = 2 * DN_QK_PER_STEP
    head0 = n_vh * pl.program_id(1)
    s = q_ref.shape[0]
    n_pairs = s // DN_PAIR

    def conv_silu(x_ref, w_ref, dst_ref, l2_scale):
        def block(i, carry):
            _conv_silu_block(x_ref, w_ref, dst_ref, l2_scale, i)
            return carry

        lax.fori_loop(0, s // CONV_ROWS, block, 0)

    conv_silu(q_ref, cwq_ref, qn_s, HEAD_DIM ** -0.5)
    conv_silu(k_ref, cwk_ref, kn_s, 1.0)
    conv_silu(v_ref, cwv_ref, vn_s, None)

    ii = _iota2((DN_PAIR, DN_PAIR), 0)
    jj = _iota2((DN_PAIR, DN_PAIR), 1)
    eye = ii == jj
    incl = ((ii < DN_CHUNK) == (jj < DN_CHUNK)) & (ii >= jj)

    def gate_row(r, vh, r0):
        return gates_ref[head0 + vh, pl.ds(r, 1), pl.ds(r0, DN_PAIR)]

    def col(x):
        return jnp.broadcast_to(x, (DN_PAIR, DN_PAIR)).T

    def mm(a, b):
        return _dot(a.astype(BF16), b.astype(BF16))

    def hs(n):
        return slice(n * HEAD_DIM, (n + 1) * HEAD_DIM)

    def local_chain(p, qh, e, q, k, v, qk, kk):
        r0 = pl.multiple_of(p * DN_PAIR, DN_PAIR)
        vh = 2 * qh + e
        gc_r = gate_row(1, vh, r0)
        beta_c = col(gate_row(0, vh, r0))
        gc_c = col(gc_r)
        gl_c = col(gate_row(2, vh, r0))
        decay = jnp.exp(jnp.where(incl, gc_c - gc_r, NEG_BIG))
        a = jnp.where(eye, 0.0, beta_c * kk * decay)
        x = jnp.where(eye, 1.0, -a)
        pw = mm(a, a)
        yield
        for _ in range(4):
            x, pw = x + mm(x, pw), mm(pw, pw)
            yield
        x = x + mm(x, pw)
        ep_c = jnp.exp(gc_c)
        rhs = jnp.concatenate([v * beta_c, k * (beta_c * ep_c)], axis=1)
        yield
        sol = mm(x, rhs)
        qd = q * ep_c
        kdt = (k * jnp.exp(gl_c)).T
        qkd = qk * decay
        yield
        u0 = sol[:, :HEAD_DIM]
        w = sol[:, HEAD_DIM:]
        for cc in range(2):
            c = 2 * p + cc
            rs = slice(cc * DN_CHUNK, (cc + 1) * DN_CHUNK)
            u0_s[vh, c] = u0[rs]
            wq_s[vh, c] = jnp.concatenate([w[rs], qd[rs]], axis=0).astype(BF16)
            qkd_s[vh, c] = qkd[rs, rs].astype(BF16)
            kdt_s[vh, c] = kdt[:, rs].astype(BF16)

    def local(it, _):
        chains = []
        for pp in range(DN_LOCAL_PAIRS):
            p = DN_LOCAL_PAIRS * it + pp
            r0 = pl.multiple_of(p * DN_PAIR, DN_PAIR)
            for qh in range(DN_QK_PER_STEP):
                q = qn_s[pl.ds(r0, DN_PAIR), hs(qh)]
                k = kn_s[pl.ds(r0, DN_PAIR), hs(qh)]
                qk2 = _dot_nt(jnp.concatenate([q, k], axis=0).astype(BF16), k.astype(BF16))
                for e in range(2):
                    v = vn_s[pl.ds(r0, DN_PAIR), hs(2 * qh + e)]
                    chains.append(local_chain(p, qh, e, q, k, v, qk2[:DN_PAIR], qk2[DN_PAIR:]))
        _emit_staggered(chains, 0)
        return 0

    lax.fori_loop(0, n_pairs // DN_LOCAL_PAIRS, local, 0)

    og = og_ref[...]

    def recur_head(vh, p, cc):
        r0 = pl.multiple_of(p * DN_PAIR, DN_PAIR)
        c = 2 * p + cc
        row0 = pl.multiple_of(c * DN_CHUNK, DN_CHUNK)
        r = _dot(wq_s[vh, c], st_s[vh].astype(BF16))
        yield
        u = (u0_s[vh, c] - r[:DN_CHUNK]).astype(BF16)
        o = r[DN_CHUNK:] + _dot(qkd_s[vh, c], u)
        st_s[vh] = st_s[vh] * gate_row(3 + cc, vh, r0) + _dot(kdt_s[vh, c], u)
        yield
        ms = jnp.mean(o * o, axis=-1, keepdims=True)
        o_ref[pl.ds(row0, DN_CHUNK), hs(vh)] = ((o * lax.rsqrt(ms + EPS)) * og).astype(o_ref.dtype)

    def recur(p, _):
        for cc in range(2):
            _emit_staggered([recur_head(vh, p, cc) for vh in range(n_vh)], 0)
        return 0

    st_s[...] = jnp.zeros_like(st_s)
    lax.fori_loop(0, n_pairs, recur, 0)


def _deltanet(proj, conv_w, gates, out_g, batch, q_off, k_off, v_off):
    m = proj.shape[0]
    s = m // batch
    heads = gates.shape[1]
    n_vh = 2 * DN_QK_PER_STEP
    steps = heads // n_vh
    n_chunks = s // DN_CHUNK
    qw = DN_QK_PER_STEP * HEAD_DIM
    vw = n_vh * HEAD_DIM
    qb, kb, vb = q_off // qw, k_off // qw, v_off // vw
    ck = (heads // 2) * HEAD_DIM // qw
    cv = heads * HEAD_DIM // vw
    return pl.pallas_call(
        _dn_kernel,
        grid=(batch, steps),
        in_specs=[
            pl.BlockSpec((s, qw), lambda b, h: (b, qb + h)),
            pl.BlockSpec((s, qw), lambda b, h: (b, kb + h)),
            pl.BlockSpec((s, vw), lambda b, h: (b, vb + h)),
            pl.BlockSpec((DN_CONV, qw), lambda b, h: (0, h)),
            pl.BlockSpec((DN_CONV, qw), lambda b, h: (0, ck + h)),
            pl.BlockSpec((DN_CONV, vw), lambda b, h: (0, cv + h)),
            pl.BlockSpec((None, heads, N_GATE_ROWS, s), lambda b, h: (b, 0, 0, 0)),
            pl.BlockSpec((1, HEAD_DIM), lambda b, h: (0, 0)),
        ],
        out_specs=pl.BlockSpec((s, vw), lambda b, h: (b, h)),
        out_shape=jax.ShapeDtypeStruct((m, heads * HEAD_DIM), BF16),
        scratch_shapes=[
            pltpu.VMEM((s, qw), F32),
            pltpu.VMEM((s, qw), F32),
            pltpu.VMEM((s, vw), F32),
            pltpu.VMEM((n_vh, n_chunks, DN_CHUNK, HEAD_DIM), F32),
            pltpu.VMEM((n_vh, n_chunks, 2 * DN_CHUNK, HEAD_DIM), BF16),
            pltpu.VMEM((n_vh, n_chunks, DN_CHUNK, DN_CHUNK), BF16),
            pltpu.VMEM((n_vh, n_chunks, HEAD_DIM, DN_CHUNK), BF16),
            pltpu.VMEM((n_vh, HEAD_DIM, HEAD_DIM), F32),
        ],
        compiler_params=_cparams(("parallel", "arbitrary")),
        name="gated_deltanet",
    )(proj, proj, proj, conv_w, conv_w, conv_w, gates, out_g.reshape(1, HEAD_DIM))


SB_TILE = 128
SB_STATIC_SWEEPS = 3
SB_DONE = -110.0
SB_SKEW = 3


SB_HEADS_PER_STEP = 2


def _sb_kernel(q_ref, k_ref, v_ref, qg_ref, kg_ref, o_ref, qn_s, kn_s, acc_s, carry_s):
    s = q_ref.shape[0]
    t = SB_TILE
    nq = s // t
    n_heads = SB_HEADS_PER_STEP

    def hs(h):
        return slice(h * HEAD_DIM, (h + 1) * HEAD_DIM)

    def normalise(x_ref, g_ref, scale, dst):
        for h in range(n_heads):
            x = x_ref[:, hs(h)].astype(F32)
            ms = jnp.mean(x * x, axis=-1, keepdims=True)
            dst[:, hs(h)] = ((x * lax.rsqrt(ms + EPS)) * g_ref[...] * scale).astype(BF16)

    normalise(q_ref, qg_ref, HEAD_DIM ** -0.5, qn_s)
    normalise(k_ref, kg_ref, 1.0, kn_s)

    ii = _iota2((t, t), 0)
    jj = _iota2((t, t), 1)
    upper = jnp.where(ii > jj, 1.0, 0.0).astype(BF16)
    causal = jj < ii

    def tile(h, q0, k0, diagonal):
        z = _dot_nt(qn_s[pl.ds(q0, t), hs(h)], kn_s[pl.ds(k0, t), hs(h)])
        yield
        log_beta = jnp.minimum(z, 0.0) - jnp.log(1.0 + jnp.exp(-jnp.abs(z)))
        log_rest = log_beta - z
        if diagonal:
            log_rest = jnp.where(causal, log_rest, 0.0)
        hi = log_rest.astype(BF16)
        lo = (log_rest - hi.astype(F32)).astype(BF16)
        later = _dot(hi, upper) + _dot(lo, upper)
        row_sum = jnp.sum(log_rest, axis=-1, keepdims=True)
        yield
        if diagonal:
            w = jnp.where(causal, jnp.exp(log_beta + later), 0.0)
            acc_s[pl.ds(q0, t), hs(h)] = _dot(w.astype(BF16), v_ref[pl.ds(k0, t), hs(h)])
            carry_s[h, pl.ds(q0, t), :] = row_sum
        else:
            carry = carry_s[h, pl.ds(q0, t), :]
            w = jnp.exp(log_beta + (later + carry))
            acc_s[pl.ds(q0, t), hs(h)] += _dot(w.astype(BF16), v_ref[pl.ds(k0, t), hs(h)])
            carry_s[h, pl.ds(q0, t), :] = carry + row_sum

    always = [tile(h, i * t, (i - d) * t, d == 0)
              for d in range(SB_STATIC_SWEEPS) for i in range(d, nq) for h in range(n_heads)]
    _emit_staggered(always, 1, SB_SKEW)

    rows = _iota2((s, 1), 0)

    def live(d):
        worst = jnp.maximum(*[jnp.max(jnp.where(rows >= d * t, carry_s[h], SB_DONE))
                              for h in range(n_heads)])
        return worst > SB_DONE

    def sweep(d, _):
        @pl.when(live(d))
        def _():
            def body(i, _):
                for h in range(n_heads):
                    for _ in tile(h, pl.multiple_of(i * t, t), pl.multiple_of((i - d) * t, t),
                                  False):
                        pass
                return 0

            lax.fori_loop(d, nq, body, 0)

        return 0

    @pl.when(live(SB_STATIC_SWEEPS))
    def _():
        lax.fori_loop(SB_STATIC_SWEEPS, nq, sweep, 0)

    o_ref[...] = acc_s[...].astype(o_ref.dtype)


def _stickbreaking(proj, q_g, k_g, batch, heads, q_off, k_off, v_off):
    m = proj.shape[0]
    s = m // batch
    hw = SB_HEADS_PER_STEP * HEAD_DIM
    qb, kb, vb = q_off // hw, k_off // hw, v_off // hw
    return pl.pallas_call(
        _sb_kernel,
        grid=(batch, heads // SB_HEADS_PER_STEP),
        in_specs=[
            pl.BlockSpec((s, hw), lambda b, h: (b, qb + h)),
            pl.BlockSpec((s, hw), lambda b, h: (b, kb + h)),
            pl.BlockSpec((s, hw), lambda b, h: (b, vb + h)),
            pl.BlockSpec((1, HEAD_DIM), lambda b, h: (0, 0)),
            pl.BlockSpec((1, HEAD_DIM), lambda b, h: (0, 0)),
        ],
        out_specs=pl.BlockSpec((s, hw), lambda b, h: (b, h)),
        out_shape=jax.ShapeDtypeStruct((m, heads * HEAD_DIM), BF16),
        scratch_shapes=[
            pltpu.VMEM((s, hw), BF16),
            pltpu.VMEM((s, hw), BF16),
            pltpu.VMEM((s, hw), F32),
            pltpu.VMEM((SB_HEADS_PER_STEP, s, 1), F32),
        ],
        compiler_params=_cparams(("parallel", "parallel")),
        name="stickbreaking_attn",
    )(proj, proj, proj, q_g.reshape(1, HEAD_DIM), k_g.reshape(1, HEAD_DIM))


def kernel(x, mem, norm_g, mem_norm_g, mem_w_kv, xa_q_norm_g, xa_k_norm_g, w_out,
           dn_w_in, dn_conv_w, dn_a_log, dn_dt_bias, dn_out_norm_g,
           sb_w_in, sb_q_norm_g, sb_k_norm_g):
    batch, seq, d = x.shape
    depth = norm_g.shape[0]
    inner = w_out.shape[1]
    mix_width = inner - XA_WIDTH
    heads = mix_width // HEAD_DIM
    m = batch * seq
    tn_dn = 1280

    dn_qk_width = (heads // 2) * HEAD_DIM
    dn_ab_off = 2 * dn_qk_width + mix_width
    dn_mix_cols = dn_ab_off + 2 * heads
    dn_mix_pad = -(-(dn_ab_off + 2 * LANES) // tn_dn) * tn_dn
    sb_mix_cols = 3 * mix_width

    kv = _memkv(mem.reshape(-1, d), mem_norm_g, mem_w_kv, xa_k_norm_g)
    w_out_b = None

    x2 = x.reshape(m, d)
    h = None
    for i in range(depth):
        j = i // 2
        src, g, normalise = (x2, norm_g[i], True) if h is None else (h, norm_g[i], False)
        if i % 2 == 0:
            w_t = jnp.swapaxes(dn_w_in[j], 0, 1)
            pmix = _inproj(src, g, w_t, dn_mix_pad, 1024, tn_dn, normalise, True)
            side = _inproj(src, g, w_t, w_t.shape[0] - dn_mix_cols, 1024, tn_dn, normalise, True,
                           col0=dn_mix_cols,
                           cast_rows=w_out.reshape(-1, d) if w_out_b is None else None)
            if w_out_b is None:
                side, w_out_cast = side
                w_out_b = w_out_cast.reshape(w_out.shape)
            xq_off = 0
            gates = _dn_gates(pmix, dn_a_log[j], dn_dt_bias[j], batch, dn_ab_off)
            mix = _deltanet(pmix, dn_conv_w[j], gates, dn_out_norm_g[j], batch,
                            0, dn_qk_width, 2 * dn_qk_width)
        else:
            if w_out_b is None:
                w_out_b = w_out.astype(BF16)
            side = _inproj(src, g, sb_w_in[j], sb_w_in.shape[2], 2048, 1024, normalise)
            xq_off = sb_mix_cols
            mix = _stickbreaking(side, sb_q_norm_g[j], sb_k_norm_g[j], batch, heads,
                                 0, mix_width, 2 * mix_width)
        xa = _xattn(side, kv, xa_q_norm_g[i], i, batch, xq_off)
        next_g = norm_g[i + 1] if i + 1 < depth else None
        res = _outproj(side, xq_off + XA_WIDTH, mix, xa, x2, w_out_b, i, next_g)
        x2, h = res if next_g is not None else (res, None)
    return x2.reshape(batch, seq, d)
```

```python
import functools

import jax
import jax.numpy as jnp
from jax import lax
from jax.experimental import pallas as pl
from jax.experimental.pallas import tpu as pltpu

F32 = jnp.float32
BF16 = jnp.bfloat16
EPS = 1e-6

XA_HEADS = 4
XA_DIM = 256
XA_WIDTH = XA_HEADS * XA_DIM
HEAD_DIM = 128
DN_CONV = 4
DN_CHUNK = 64
DN_PAIR = 2 * DN_CHUNK
N_GATE_ROWS = 8

LANES = 128
VMEM_BYTES_V7X = 64 * 1024 * 1024
VMEM_LIMIT = VMEM_BYTES_V7X - 8 * 1024 * 1024

NEG_BIG = -1e30


def _cparams(sem):
    return pltpu.CompilerParams(dimension_semantics=sem, vmem_limit_bytes=VMEM_LIMIT)


def _dot(a, b):
    return jnp.dot(a, b, preferred_element_type=F32)


def _dot_nt(a, b):
    return lax.dot_general(a, b, (((1,), (1,)), ((), ())), preferred_element_type=F32)


def _split3(x):
    hi = x.astype(BF16)
    r = x - hi.astype(F32)
    mid = r.astype(BF16)
    lo = (r - mid.astype(F32)).astype(BF16)
    return hi, mid, lo


def _sigmoid(x):
    return 1.0 / (1.0 + jnp.exp(-x))


def _softplus(x):
    return jnp.maximum(x, 0.0) + jnp.log(1.0 + jnp.exp(-jnp.abs(x)))


def _iota2(shape, dim):
    return lax.broadcasted_iota(jnp.int32, shape, dim)


def _emit_staggered(items, lag, skew=1):
    live = list(enumerate(items))
    stage = {n: 0 for n, _ in live}
    t = 0
    while live:
        due = [(n, g) for n, g in live if n * lag + stage[n] * skew <= t]
        for n, g in sorted(due, key=lambda ng: -stage[ng[0]]):
            try:
                next(g)
                stage[n] += 1
            except StopIteration:
                live.remove((n, g))
        t += 1


def _rms_rows(x, g):
    ms = jnp.mean(x * x, axis=-1, keepdims=True)
    return (x * lax.rsqrt(ms + EPS)) * g


def _inproj_kernel(*refs, normalise, w_transposed, with_cast):
    if with_cast:
        x_ref, g_ref, w_ref, cast_ref, o_ref, cast_o_ref, wb_s = refs
        cast_o_ref[...] = cast_ref[...].astype(BF16)
    else:
        x_ref, g_ref, w_ref, o_ref, wb_s = refs

    @pl.when(pl.program_id(1) == 0)
    def _():
        wb_s[...] = w_ref[...].astype(BF16)

    if normalise:
        h = _rms_rows(x_ref[...], g_ref[...]).astype(BF16)
    else:
        h = x_ref[...]
    acc = _dot_nt(h, wb_s[...]) if w_transposed else _dot(h, wb_s[...])
    o_ref[...] = acc.astype(o_ref.dtype)


def _inproj(x2, g, w, cols, tm, tn, normalise, w_transposed=False, col0=0, cast_rows=None):
    m, d = x2.shape
    if w_transposed:
        w_block = (tn, d)
        w_spec = pl.BlockSpec((pl.Element(tn), pl.Element(d)),
                              lambda j, i: (pl.multiple_of(col0 + j * tn, 8), 0))
    else:
        assert col0 == 0
        w_block, w_spec = (d, tn), pl.BlockSpec((d, tn), lambda j, i: (0, j))
    n_j, n_i = cols // tn, m // tm
    in_specs = [
        pl.BlockSpec((tm, d), lambda j, i: (i, 0)),
        pl.BlockSpec((1, d), lambda j, i: (0, 0)),
        w_spec,
    ]
    operands = [x2, g.reshape(1, d), w]
    out_specs = pl.BlockSpec((tm, tn), lambda j, i: (i, j))
    out_shape = jax.ShapeDtypeStruct((m, cols), BF16)
    if cast_rows is not None:
        r, c = cast_rows.shape
        slab = r // (n_j * n_i)
        assert slab * n_j * n_i == r
        slab_spec = pl.BlockSpec((slab, c), lambda j, i: (j * n_i + i, 0))
        in_specs.append(slab_spec)
        operands.append(cast_rows)
        out_specs = (out_specs, slab_spec)
        out_shape = (out_shape, jax.ShapeDtypeStruct((r, c), BF16))
    return pl.pallas_call(
        functools.partial(_inproj_kernel, normalise=normalise, w_transposed=w_transposed,
                          with_cast=cast_rows is not None),
        grid=(n_j, n_i),
        in_specs=in_specs,
        out_specs=out_specs,
        out_shape=out_shape,
        scratch_shapes=[pltpu.VMEM(w_block, BF16)],
        compiler_params=_cparams(("arbitrary", "arbitrary")),
        name="rmsnorm_inproj",
    )(*operands)


def _memkv_kernel(mem_ref, g_ref, w_ref, kg_ref, o_ref, mn_s, *, k_tiles):
    @pl.when((pl.program_id(0) == 0) & (pl.program_id(1) == 0))
    def _():
        mn_s[...] = _rms_rows(mem_ref[...], g_ref[...]).astype(BF16)

    kv = _dot(mn_s[...], w_ref[0].astype(BF16))
    is_key = pl.program_id(1) < k_tiles

    @pl.when(is_key)
    def _():
        for c in range(kv.shape[1] // XA_DIM):
            kh = kv[:, c * XA_DIM:(c + 1) * XA_DIM]
            hs = jnp.mean(kh * kh, axis=-1, keepdims=True)
            o_ref[0, :, c * XA_DIM:(c + 1) * XA_DIM] = (
                (kh * lax.rsqrt(hs + EPS)) * kg_ref[0]).astype(o_ref.dtype)

    @pl.when(jnp.logical_not(is_key))
    def _():
        o_ref[0] = kv.astype(o_ref.dtype)


def _memkv(mem2, mem_g, w_kv, k_g, tn=512):
    depth, d, e = w_kv.shape
    mm = mem2.shape[0]
    return pl.pallas_call(
        functools.partial(_memkv_kernel, k_tiles=XA_WIDTH // tn),
        grid=(depth, e // tn),
        in_specs=[
            pl.BlockSpec((mm, d), lambda l, j: (0, 0)),
            pl.BlockSpec((1, d), lambda l, j: (0, 0)),
            pl.BlockSpec((1, d, tn), lambda l, j: (l, 0, j)),
            pl.BlockSpec((1, 1, XA_DIM), lambda l, j: (l, 0, 0)),
        ],
        out_specs=pl.BlockSpec((1, mm, tn), lambda l, j: (l, 0, j)),
        out_shape=jax.ShapeDtypeStruct((depth, mm, e), BF16),
        scratch_shapes=[pltpu.VMEM((mm, d), BF16)],
        compiler_params=_cparams(("arbitrary", "arbitrary")),
        name="mem_kv",
    )(mem2, mem_g.reshape(1, d), w_kv, k_g.reshape(depth, 1, XA_DIM))


def _xattn_kernel(q_ref, qg_ref, k_ref, v_ref, o_ref):
    def head(h):
        cs = slice(h * XA_DIM, (h + 1) * XA_DIM)
        qn = _rms_rows(q_ref[:, cs].astype(F32), qg_ref[...]).astype(BF16)
        yield
        s = _dot_nt(qn, k_ref[0, :, cs]) * (XA_DIM ** -0.5)
        yield
        e = jnp.exp(s - jnp.max(s, axis=-1, keepdims=True))
        p = (e / jnp.sum(e, axis=-1, keepdims=True)).astype(BF16)
        yield
        o_ref[:, cs] = _dot(p, v_ref[0, :, cs]).astype(o_ref.dtype)

    _emit_staggered([head(h) for h in range(XA_HEADS)], 1)


def _xattn(proj, kv, q_g, layer, batch, xq_off, tm=512):
    m = proj.shape[0]
    n_mem = kv.shape[1] // batch
    nt = m // batch // tm
    qb = xq_off // XA_WIDTH
    return pl.pallas_call(
        _xattn_kernel,
        grid=(batch, nt),
        in_specs=[
            pl.BlockSpec((tm, XA_WIDTH), lambda b, i: (b * nt + i, qb)),
            pl.BlockSpec((1, XA_DIM), lambda b, i: (0, 0)),
            pl.BlockSpec((1, n_mem, XA_WIDTH), lambda b, i: (layer, b, 0)),
            pl.BlockSpec((1, n_mem, XA_WIDTH), lambda b, i: (layer, b, 1)),
        ],
        out_specs=pl.BlockSpec((tm, XA_WIDTH), lambda b, i: (b * nt + i, 0)),
        out_shape=jax.ShapeDtypeStruct((m, XA_WIDTH), BF16),
        compiler_params=_cparams(("parallel", "parallel")),
        name="mem_xattn",
    )(proj, q_g.reshape(1, XA_DIM), kv, kv)


def _outproj_kernel(*refs, emit_next):
    z_refs = refs[:4]
    mix_ref, xa_ref, x_ref, w_ref = refs[4:8]
    if emit_next:
        gn_ref, o_ref, hn_ref, y_s = refs[8:]
    else:
        o_ref, y_s = refs[8:]
    zw = z_refs[0].shape[1]
    n_mix = mix_ref.shape[1] // zw
    for c, z_ref in enumerate(z_refs):
        z = z_ref[...].astype(F32)
        if c < n_mix:
            br = mix_ref[:, c * zw:(c + 1) * zw]
        else:
            br = xa_ref[:, (c - n_mix) * zw:(c - n_mix + 1) * zw]
        y_s[:, c * zw:(c + 1) * zw] = (br.astype(F32) * (z * _sigmoid(z))).astype(BF16)
    out = x_ref[...] + _dot(y_s[...], w_ref[...])
    o_ref[...] = out
    if emit_next:
        hn_ref[...] = _rms_rows(out, gn_ref[...]).astype(BF16)


def _outproj(proj, z_off, mix, xa, x2, w, layer, next_g, tm=256):
    m, d = x2.shape
    inner = w.shape[1]
    mw = mix.shape[1]
    zw = inner // 4
    zb = z_off // zw
    emit_next = next_g is not None
    row_spec = pl.BlockSpec((tm, d), lambda i: (i, 0))
    in_specs = [pl.BlockSpec((tm, zw), functools.partial(lambda c, i: (i, zb + c), c))
                for c in range(4)]
    in_specs += [
        pl.BlockSpec((tm, mw), lambda i: (i, 0)),
        pl.BlockSpec((tm, inner - mw), lambda i: (i, 0)),
        row_spec,
        pl.BlockSpec((None, inner, d), lambda i: (layer, 0, 0), pipeline_mode=pl.Buffered(1)),
    ]
    operands = [proj, proj, proj, proj, mix, xa, x2, w]
    out_specs, out_shape = row_spec, jax.ShapeDtypeStruct((m, d), F32)
    if emit_next:
        in_specs.append(pl.BlockSpec((1, d), lambda i: (0, 0)))
        operands.append(next_g.reshape(1, d))
        out_specs = (row_spec, row_spec)
        out_shape = (out_shape, jax.ShapeDtypeStruct((m, d), BF16))
    return pl.pallas_call(
        functools.partial(_outproj_kernel, emit_next=emit_next),
        grid=(m // tm,),
        in_specs=in_specs,
        out_specs=out_specs,
        out_shape=out_shape,
        scratch_shapes=[pltpu.VMEM((tm, inner), BF16)],
        compiler_params=_cparams(("parallel",)),
        name="gate_outproj",
    )(*operands)


def _dn_gates_kernel(ab_ref, alog_ref, dtb_ref, o_ref, *, heads):
    s = ab_ref.shape[0]
    t = ab_ref[:, :LANES].astype(F32).T
    a = t[0:heads]
    b = t[heads:2 * heads]
    g = -jnp.exp(alog_ref[...]) * _softplus(a + dtb_ref[...])
    beta = _sigmoid(b)

    ii = _iota2((DN_PAIR, DN_PAIR), 0)
    jj = _iota2((DN_PAIR, DN_PAIR), 1)
    lo_i = ii < DN_CHUNK
    lo_j = jj < DN_CHUNK
    same = lo_i == lo_j
    one = lambda m: jnp.where(m, 1.0, 0.0).astype(BF16)
    rhs = jnp.concatenate(
        [one(same & (ii <= jj)), one(same), one(lo_i), one(jnp.logical_not(lo_i))], axis=1)

    for p in range(s // DN_PAIR):
        ls = slice(p * DN_PAIR, (p + 1) * DN_PAIR)
        hi, mid, lo = _split3(g[:, ls])
        r = _dot(hi, rhs) + _dot(mid, rhs) + _dot(lo, rhs)
        gc = r[:, 0:DN_PAIR]
        rows = (beta[:, ls], gc, r[:, DN_PAIR:2 * DN_PAIR] - gc,
                jnp.exp(r[:, 2 * DN_PAIR:3 * DN_PAIR]), jnp.exp(r[:, 3 * DN_PAIR:4 * DN_PAIR]))
        for n, val in enumerate(rows):
            o_ref[:, n, ls] = val
        for n in range(len(rows), N_GATE_ROWS):
            o_ref[:, n, ls] = jnp.zeros_like(gc)


def _dn_gates(proj, a_log, dt_bias, batch, ab_off):
    m = proj.shape[0]
    s = m // batch
    heads = a_log.shape[0]
    blk = 2 * LANES
    return pl.pallas_call(
        functools.partial(_dn_gates_kernel, heads=heads),
        grid=(batch,),
        in_specs=[
            pl.BlockSpec((s, blk), lambda b: (b, ab_off // blk)),
            pl.BlockSpec((heads, 1), lambda b: (0, 0)),
            pl.BlockSpec((heads, 1), lambda b: (0, 0)),
        ],
        out_specs=pl.BlockSpec((None, heads, N_GATE_ROWS, s), lambda b: (b, 0, 0, 0)),
        out_shape=jax.ShapeDtypeStruct((batch, heads, N_GATE_ROWS, s), F32),
        compiler_params=_cparams(("parallel",)),
        name="dn_gates",
    )(proj, a_log.reshape(heads, 1), dt_bias.reshape(heads, 1))


CONV_ROWS = 256


def _conv_silu_block(x_ref, w_ref, dst_ref, l2_scale, i):
    width = x_ref.shape[1]
    w = w_ref[...]
    row8 = _iota2((8, width), 0)
    r0 = pl.multiple_of(i * CONV_ROWS, CONV_ROWS)
    x = x_ref[pl.ds(r0, CONV_ROWS), :].astype(F32)
    p0 = pl.multiple_of(jnp.maximum(r0 - 16, 0), 16)
    prev = x_ref[pl.ds(p0, 16), :].astype(F32)[8:16]
    prev = prev * jnp.where(i > 0, 1.0, 0.0)
    acc = x * w[DN_CONV - 1:DN_CONV]
    for k in range(1, DN_CONV):
        xr = pltpu.roll(x, k, axis=0)
        pr = pltpu.roll(prev, k, axis=0)
        head = jnp.where(row8 < k, pr, xr[0:8])
        xr = jnp.concatenate([head, xr[8:]], axis=0)
        acc = acc + xr * w[DN_CONV - 1 - k:DN_CONV - k]
    y = acc * _sigmoid(acc)
    if l2_scale is not None:
        for c in range(width // HEAD_DIM):
            cs = slice(c * HEAD_DIM, (c + 1) * HEAD_DIM)
            yc = y[:, cs]
            ss = jnp.sum(yc * yc, axis=-1, keepdims=True)
            dst_ref[pl.ds(r0, CONV_ROWS), cs] = (yc * lax.rsqrt(ss + EPS)) * l2_scale
    else:
        dst_ref[pl.ds(r0, CONV_ROWS), :] = y


DN_QK_PER_STEP = 2
DN_LOCAL_PAIRS = 2


def _dn_kernel(q_ref, k_ref, v_ref, cwq_ref, cwk_ref, cwv_ref, gates_ref, og_ref, o_ref,
               qn_s, kn_s, vn_s, u0_s, wq_s, qkd_s, kdt_s, st_s):
    n_vh = 2 * DN_QK_PER_STEP
    head0 = n_vh * pl.program_id(1)
    s = q_ref.shape[0]
    n_pairs = s // DN_PAIR

    def conv_silu(x_ref, w_ref, dst_ref, l2_scale):
        def block(i, carry):
            _conv_silu_block(x_ref, w_ref, dst_ref, l2_scale, i)
            return carry

        lax.fori_loop(0, s // CONV_ROWS, block, 0)

    conv_silu(q_ref, cwq_ref, qn_s, HEAD_DIM ** -0.5)
    conv_silu(k_ref, cwk_ref, kn_s, 1.0)
    conv_silu(v_ref, cwv_ref, vn_s, None)

    ii = _iota2((DN_PAIR, DN_PAIR), 0)
    jj = _iota2((DN_PAIR, DN_PAIR), 1)
    eye = ii == jj
    incl = ((ii < DN_CHUNK) == (jj < DN_CHUNK)) & (ii >= jj)

    def gate_row(r, vh, r0):
        return gates_ref[head0 + vh, pl.ds(r, 1), pl.ds(r0, DN_PAIR)]

    def col(x):
        return jnp.broadcast_to(x, (DN_PAIR, DN_PAIR)).T

    def mm(a, b):
        return _dot(a.astype(BF16), b.astype(BF16))

    def hs(n):
        return slice(n * HEAD_DIM, (n + 1) * HEAD_DIM)

    def local_chain(p, qh, e, q, k, v, qk, kk):
        r0 = pl.multiple_of(p * DN_PAIR, DN_PAIR)
        vh = 2 * qh + e
        gc_r = gate_row(1, vh, r0)
        beta_c = col(gate_row(0, vh, r0))
        gc_c = col(gc_r)
        gl_c = col(gate_row(2, vh, r0))
        decay = jnp.exp(jnp.where(incl, gc_c - gc_r, NEG_BIG))
        a = jnp.where(eye, 0.0, beta_c * kk * decay)
        x = jnp.where(eye, 1.0, -a)
        pw = mm(a, a)
        yield
        for _ in range(4):
            x, pw = x + mm(x, pw), mm(pw, pw)
            yield
        x = x + mm(x, pw)
        ep_c = jnp.exp(gc_c)
        rhs = jnp.concatenate([v * beta_c, k * (beta_c * ep_c)], axis=1)
        yield
        sol = mm(x, rhs)
        qd = q * ep_c
        kdt = (k * jnp.exp(gl_c)).T
        qkd = qk * decay
        yield
        u0 = sol[:, :HEAD_DIM]
        w = sol[:, HEAD_DIM:]
        for cc in range(2):
            c = 2 * p + cc
            rs = slice(cc * DN_CHUNK, (cc + 1) * DN_CHUNK)
            u0_s[vh, c] = u0[rs]
            wq_s[vh, c] = jnp.concatenate([w[rs], qd[rs]], axis=0).astype(BF16)
            qkd_s[vh, c] = qkd[rs, rs].astype(BF16)
            kdt_s[vh, c] = kdt[:, rs].astype(BF16)

    def local(it, _):
        chains = []
        for pp in range(DN_LOCAL_PAIRS):
            p = DN_LOCAL_PAIRS * it + pp
            r0 = pl.multiple_of(p * DN_PAIR, DN_PAIR)
            for qh in range(DN_QK_PER_STEP):
                q = qn_s[pl.ds(r0, DN_PAIR), hs(qh)]
                k = kn_s[pl.ds(r0, DN_PAIR), hs(qh)]
                qk2 = _dot_nt(jnp.concatenate([q, k], axis=0).astype(BF16), k.astype(BF16))
                for e in range(2):
                    v = vn_s[pl.ds(r0, DN_PAIR), hs(2 * qh + e)]
                    chains.append(local_chain(p, qh, e, q, k, v, qk2[:DN_PAIR], qk2[DN_PAIR:]))
        _emit_staggered(chains, 0)
        return 0

    lax.fori_loop(0, n_pairs // DN_LOCAL_PAIRS, local, 0)

    og = og_ref[...]

    def recur_head(vh, p, cc):
        r0 = pl.multiple_of(p * DN_PAIR, DN_PAIR)
        c = 2 * p + cc
        row0 = pl.multiple_of(c * DN_CHUNK, DN_CHUNK)
        r = _dot(wq_s[vh, c], st_s[vh].astype(BF16))
        yield
        u = (u0_s[vh, c] - r[:DN_CHUNK]).astype(BF16)
        o = r[DN_CHUNK:] + _dot(qkd_s[vh, c], u)
        st_s[vh] = st_s[vh] * gate_row(3 + cc, vh, r0) + _dot(kdt_s[vh, c], u)
        yield
        ms = jnp.mean(o * o, axis=-1, keepdims=True)
        o_ref[pl.ds(row0, DN_CHUNK), hs(vh)] = ((o * lax.rsqrt(ms + EPS)) * og).astype(o_ref.dtype)

    def recur(p, _):
        for cc in range(2):
            _emit_staggered([recur_head(vh, p, cc) for vh in range(n_vh)], 0)
        return 0

    st_s[...] = jnp.zeros_like(st_s)
    lax.fori_loop(0, n_pairs, recur, 0)


def _deltanet(proj, conv_w, gates, out_g, batch, q_off, k_off, v_off):
    m = proj.shape[0]
    s = m // batch
    heads = gates.shape[1]
    n_vh = 2 * DN_QK_PER_STEP
    steps = heads // n_vh
    n_chunks = s // DN_CHUNK
    qw = DN_QK_PER_STEP * HEAD_DIM
    vw = n_vh * HEAD_DIM
    qb, kb, vb = q_off // qw, k_off // qw, v_off // vw
    ck = (heads // 2) * HEAD_DIM // qw
    cv = heads * HEAD_DIM // vw
    return pl.pallas_call(
        _dn_kernel,
        grid=(batch, steps),
        in_specs=[
            pl.BlockSpec((s, qw), lambda b, h: (b, qb + h)),
            pl.BlockSpec((s, qw), lambda b, h: (b, kb + h)),
            pl.BlockSpec((s, vw), lambda b, h: (b, vb + h)),
            pl.BlockSpec((DN_CONV, qw), lambda b, h: (0, h)),
            pl.BlockSpec((DN_CONV, qw), lambda b, h: (0, ck + h)),
            pl.BlockSpec((DN_CONV, vw), lambda b, h: (0, cv + h)),
            pl.BlockSpec((None, heads, N_GATE_ROWS, s), lambda b, h: (b, 0, 0, 0)),
            pl.BlockSpec((1, HEAD_DIM), lambda b, h: (0, 0)),
        ],
        out_specs=pl.BlockSpec((s, vw), lambda b, h: (b, h)),
        out_shape=jax.ShapeDtypeStruct((m, heads * HEAD_DIM), BF16),
        scratch_shapes=[
            pltpu.VMEM((s, qw), F32),
            pltpu.VMEM((s, qw), F32),
            pltpu.VMEM((s, vw), F32),
            pltpu.VMEM((n_vh, n_chunks, DN_CHUNK, HEAD_DIM), F32),
            pltpu.VMEM((n_vh, n_chunks, 2 * DN_CHUNK, HEAD_DIM), BF16),
            pltpu.VMEM((n_vh, n_chunks, DN_CHUNK, DN_CHUNK), BF16),
            pltpu.VMEM((n_vh, n_chunks, HEAD_DIM, DN_CHUNK), BF16),
            pltpu.VMEM((n_vh, HEAD_DIM, HEAD_DIM), F32),
        ],
        compiler_params=_cparams(("parallel", "arbitrary")),
        name="gated_deltanet",
    )(proj, proj, proj, conv_w, conv_w, conv_w, gates, out_g.reshape(1, HEAD_DIM))


SB_TILE = 128
SB_STATIC_SWEEPS = 3
SB_DONE = -110.0
SB_SKEW = 3


SB_HEADS_PER_STEP = 2


def _sb_kernel(q_ref, k_ref, v_ref, qg_ref, kg_ref, o_ref, qn_s, kn_s, acc_s, carry_s):
    s = q_ref.shape[0]
    t = SB_TILE
    nq = s // t
    n_heads = SB_HEADS_PER_STEP

    def hs(h):
        return slice(h * HEAD_DIM, (h + 1) * HEAD_DIM)

    def normalise(x_ref, g_ref, scale, dst):
        for h in range(n_heads):
            x = x_ref[:, hs(h)].astype(F32)
            ms = jnp.mean(x * x, axis=-1, keepdims=True)
            dst[:, hs(h)] = ((x * lax.rsqrt(ms + EPS)) * g_ref[...] * scale).astype(BF16)

    normalise(q_ref, qg_ref, HEAD_DIM ** -0.5, qn_s)
    normalise(k_ref, kg_ref, 1.0, kn_s)

    ii = _iota2((t, t), 0)
    jj = _iota2((t, t), 1)
    upper = jnp.where(ii > jj, 1.0, 0.0).astype(BF16)
    causal = jj < ii

    def tile(h, q0, k0, diagonal):
        z = _dot_nt(qn_s[pl.ds(q0, t), hs(h)], kn_s[pl.ds(k0, t), hs(h)])
        yield
        log_beta = jnp.minimum(z, 0.0) - jnp.log(1.0 + jnp.exp(-jnp.abs(z)))
        log_rest = log_beta - z
        if diagonal:
            log_rest = jnp.where(causal, log_rest, 0.0)
        hi = log_rest.astype(BF16)
        lo = (log_rest - hi.astype(F32)).astype(BF16)
        later = _dot(hi, upper) + _dot(lo, upper)
        row_sum = jnp.sum(log_rest, axis=-1, keepdims=True)
        yield
        if diagonal:
            w = jnp.where(causal, jnp.exp(log_beta + later), 0.0)
            acc_s[pl.ds(q0, t), hs(h)] = _dot(w.astype(BF16), v_ref[pl.ds(k0, t), hs(h)])
            carry_s[h, pl.ds(q0, t), :] = row_sum
        else:
            carry = carry_s[h, pl.ds(q0, t), :]
            w = jnp.exp(log_beta + (later + carry))
            acc_s[pl.ds(q0, t), hs(h)] += _dot(w.astype(BF16), v_ref[pl.ds(k0, t), hs(h)])
            carry_s[h, pl.ds(q0, t), :] = carry + row_sum

    always = [tile(h, i * t, (i - d) * t, d == 0)
              for d in range(SB_STATIC_SWEEPS) for i in range(d, nq) for h in range(n_heads)]
    _emit_staggered(always, 1, SB_SKEW)

    rows = _iota2((s, 1), 0)

    def live(d):
        worst = jnp.maximum(*[jnp.max(jnp.where(rows >= d * t, carry_s[h], SB_DONE))
                              for h in range(n_heads)])
        return worst > SB_DONE

    def sweep(d, _):
        @pl.when(live(d))
        def _():
            def body(i, _):
                for h in range(n_heads):
                    for _ in tile(h, pl.multiple_of(i * t, t), pl.multiple_of((i - d) * t, t),
                                  False):
                        pass
                return 0

            lax.fori_loop(d, nq, body, 0)

        return 0

    @pl.when(live(SB_STATIC_SWEEPS))
    def _():
        lax.fori_loop(SB_STATIC_SWEEPS, nq, sweep, 0)

    o_ref[...] = acc_s[...].astype(o_ref.dtype)


def _stickbreaking(proj, q_g, k_g, batch, heads, q_off, k_off, v_off):
    m = proj.shape[0]
    s = m // batch
    hw = SB_HEADS_PER_STEP * HEAD_DIM
    qb, kb, vb = q_off // hw, k_off // hw, v_off // hw
    return pl.pallas_call(
        _sb_kernel,
        grid=(batch, heads // SB_HEADS_PER_STEP),
        in_specs=[
            pl.BlockSpec((s, hw), lambda b, h: (b, qb + h)),
            pl.BlockSpec((s, hw), lambda b, h: (b, kb + h)),
            pl.BlockSpec((s, hw), lambda b, h: (b, vb + h)),
            pl.BlockSpec((1, HEAD_DIM), lambda b, h: (0, 0)),
            pl.BlockSpec((1, HEAD_DIM), lambda b, h: (0, 0)),
        ],
        out_specs=pl.BlockSpec((s, hw), lambda b, h: (b, h)),
        out_shape=jax.ShapeDtypeStruct((m, heads * HEAD_DIM), BF16),
        scratch_shapes=[
            pltpu.VMEM((s, hw), BF16),
            pltpu.VMEM((s, hw), BF16),
            pltpu.VMEM((s, hw), F32),
            pltpu.VMEM((SB_HEADS_PER_STEP, s, 1), F32),
        ],
        compiler_params=_cparams(("parallel", "parallel")),
        name="stickbreaking_attn",
    )(proj, proj, proj, q_g.reshape(1, HEAD_DIM), k_g.reshape(1, HEAD_DIM))


def kernel(x, mem, norm_g, mem_norm_g, mem_w_kv, xa_q_norm_g, xa_k_norm_g, w_out,
           dn_w_in, dn_conv_w, dn_a_log, dn_dt_bias, dn_out_norm_g,
           sb_w_in, sb_q_norm_g, sb_k_norm_g):
    batch, seq, d = x.shape
    depth = norm_g.shape[0]
    inner = w_out.shape[1]
    mix_width = inner - XA_WIDTH
    heads = mix_width // HEAD_DIM
    m = batch * seq
    tn_dn = 1280

    dn_qk_width = (heads // 2) * HEAD_DIM
    dn_ab_off = 2 * dn_qk_width + mix_width
    dn_mix_cols = dn_ab_off + 2 * heads
    dn_mix_pad = -(-(dn_ab_off + 2 * LANES) // tn_dn) * tn_dn
    sb_mix_cols = 3 * mix_width

    kv = _memkv(mem.reshape(-1, d), mem_norm_g, mem_w_kv, xa_k_norm_g)
    w_out_b = None

    x2 = x.reshape(m, d)
    h = None
    for i in range(depth):
        j = i // 2
        src, g, normalise = (x2, norm_g[i], True) if h is None else (h, norm_g[i], False)
        if i % 2 == 0:
            w_t = jnp.swapaxes(dn_w_in[j], 0, 1)
            pmix = _inproj(src, g, w_t, dn_mix_pad, 1024, tn_dn, normalise, True)
            side = _inproj(src, g, w_t, w_t.shape[0] - dn_mix_cols, 1024, tn_dn, normalise, True,
                           col0=dn_mix_cols,
                           cast_rows=w_out.reshape(-1, d) if w_out_b is None else None)
            if w_out_b is None:
                side, w_out_cast = side
                w_out_b = w_out_cast.reshape(w_out.shape)
            xq_off = 0
            gates = _dn_gates(pmix, dn_a_log[j], dn_dt_bias[j], batch, dn_ab_off)
            mix = _deltanet(pmix, dn_conv_w[j], gates, dn_out_norm_g[j], batch,
                            0, dn_qk_width, 2 * dn_qk_width)
        else:
            if w_out_b is None:
                w_out_b = w_out.astype(BF16)
            side = _inproj(src, g, sb_w_in[j], sb_w_in.shape[2], 2048, 1024, normalise)
            xq_off = sb_mix_cols
            mix = _stickbreaking(side, sb_q_norm_g[j], sb_k_norm_g[j], batch, heads,
                                 0, mix_width, 2 * mix_width)
        xa = _xattn(side, kv, xa_q_norm_g[i], i, batch, xq_off)
        next_g = norm_g[i + 1] if i + 1 < depth else None
        res = _outproj(side, xq_off + XA_WIDTH, mix, xa, x2, w_out_b, i, next_g)
        x2, h = res if next_g is not None else (res, None)
    return x2.reshape(batch, seq, d)
```
